```python
import math
import jax, jax.numpy as jnp
from jax import lax
import numpy as np

D_MODEL = 2048
BATCH = 4
SEQ = 8192
DEPTH = 1

GRID_W = 64
CTX_LEN = 256
MIX_W = D_MODEL
GDN_W = MIX_W // 2
HYENA_W = MIX_W - GDN_W
GDN_HEADS = 8
GDN_HEAD_DIM = GDN_W // GDN_HEADS
GDN_CONV = 5
GDN_CHUNK = 64
HYENA_CONV = 3
HYENA_EMB = 33
HYENA_FILTER_W = 64
HYENA_DECAY_TARGET = 1e-2
HYENA_FAST_PCT = 0.3
HYENA_SLOW_PCT = 1.5
D_FF = ((8 * D_MODEL // 3 + 255) // 256) * 256
EPS = 1e-6

COL_K = 0
COL_V = COL_K + GDN_W
COL_Q = COL_V + GDN_W
COL_BETA = COL_Q + GDN_W
COL_Z = COL_BETA + 4 * GDN_HEADS
COL_HY = COL_Z + GDN_W
IN_COLS = COL_HY + 3 * HYENA_W

kernel_name = "hybrid_gdn_hyena_prefix_dit_block"


def _rmsnorm(x, g):
    xf = x.astype(jnp.float32)
    y = xf * lax.rsqrt(jnp.mean(xf * xf, axis=-1, keepdims=True) + EPS)
    return (y * g.astype(jnp.float32)).astype(x.dtype)


def _l2norm(x):
    xf = x.astype(jnp.float32)
    return xf * lax.rsqrt(jnp.sum(xf * xf, axis=-1, keepdims=True) + EPS)


def _dwconv(u, w):
    K = w.shape[0]
    pad = K // 2
    L = u.shape[1]
    up = jnp.pad(u, ((0, 0), (pad, pad), (0, 0)))
    out = up[:, 0:L] * w[0]
    for i in range(1, K):
        out = out + up[:, i:i + L] * w[i]
    return out


def _adaln(cvec, w_mod, b_mod):
    m = jax.nn.silu(cvec) @ w_mod + b_mod
    return jnp.split(m[..., None, :], 6, axis=-1)


def _heads(t):
    return t.reshape(t.shape[:-1] + (GDN_HEADS, GDN_HEAD_DIM))


def _gdn_gates(ba, a_log, dt_bias):
    ba = ba.astype(jnp.float32).reshape(ba.shape[:-1] + (2, 2, GDN_HEADS))
    beta = jax.nn.sigmoid(ba[..., 0, :, :])
    g = -jnp.exp(a_log.astype(jnp.float32)) * jax.nn.softplus(ba[..., 1, :, :] + dt_bias.astype(jnp.float32))
    return beta, g


def _to_chunks(t):
    B, L, H = t.shape[:3]
    t = t.reshape((B, L // GDN_CHUNK, GDN_CHUNK, H) + t.shape[3:])
    return jnp.moveaxis(t, (1, 3), (0, 2))


def _gdn_chunks(k, v, beta, g):
    kc, vc, bc, gc = _to_chunks(k), _to_chunks(v), _to_chunks(beta), _to_chunks(g)
    gcum = jnp.cumsum(gc, axis=-1)
    idx = jnp.arange(GDN_CHUNK)
    strict = idx[:, None] > idx[None, :]
    diff = jnp.where(strict, gcum[..., :, None] - gcum[..., None, :], 0.0)
    kk = jnp.einsum('nbhid,nbhjd->nbhij', kc, kc)
    a_strict = jnp.where(strict, bc[..., :, None] * kk * jnp.exp(diff), 0.0)
    lower_unit = a_strict + jnp.eye(GDN_CHUNK, dtype=a_strict.dtype)
    rhs = jnp.concatenate([vc * bc[..., None], kc * (bc * jnp.exp(gcum))[..., None]], axis=-1)
    sol = lax.linalg.triangular_solve(lower_unit, rhs, left_side=True, lower=True, unit_diagonal=True)
    dv = vc.shape[-1]
    return kc, gcum, sol[..., :dv], sol[..., dv:]


def _gdn_scan(kc, gcum, u, w, s0, emit):
    def step(s, xs):
        k_n, g_n, u_n, w_n = xs
        v_new = u_n - jnp.einsum('bhcd,bhde->bhce', w_n, s)
        g_last = g_n[..., -1]
        s_next = s * jnp.exp(g_last)[..., None, None] + jnp.einsum(
            'bhcd,bhce->bhde', k_n * jnp.exp(g_last[..., None] - g_n)[..., None], v_new)
        return s_next, ((s, v_new) if emit else None)
    return lax.scan(step, s0, (kc, gcum, u, w))


def _gdn_direction(q, k, v, beta, g, s0):
    kc, gcum, u, w = _gdn_chunks(k, v, beta, g)
    s_fin, (s_start, v_new) = _gdn_scan(kc, gcum, u, w, s0, True)
    qc = _to_chunks(q)
    idx = jnp.arange(GDN_CHUNK)
    incl = idx[:, None] >= idx[None, :]
    diff = jnp.where(incl, gcum[..., :, None] - gcum[..., None, :], 0.0)
    attn = jnp.where(incl, jnp.einsum('nbhid,nbhjd->nbhij', qc, kc) * jnp.exp(diff), 0.0)
    o = (jnp.einsum('nbhcd,nbhde->nbhce', qc * jnp.exp(gcum)[..., None], s_start)
         + jnp.einsum('nbhij,nbhje->nbhie', attn, v_new))
    N, B, H, C, E = o.shape
    return jnp.moveaxis(o, (0, 2), (1, 3)).reshape(B, N * C, H, E), s_fin


def _gdn_final_state(k, v, beta, g, s0):
    kc, gcum, u, w = _gdn_chunks(k, v, beta, g)
    s_fin, _ = _gdn_scan(kc, gcum, u, w, s0, False)
    return s_fin


def _hyena_filter(L, w1, b1, f1, w2, b2, f2, w3, b3, f3, w4):
    f32 = jnp.float32
    t = jnp.arange(L, dtype=f32)
    t01 = t / max(L - 1, 1)
    bands = (HYENA_EMB - 1) // 2
    freqs = jnp.linspace(1e-4, bands - 1, bands, dtype=f32)
    ang = (2.0 * math.pi / L) * t[:, None] * freqs[None, :]
    z = jnp.concatenate([t01[:, None], jnp.cos(ang), -jnp.sin(ang)], axis=-1)
    h = jnp.sin(f1.astype(f32) * (z @ w1.astype(f32) + b1.astype(f32)))
    h = jnp.sin(f2.astype(f32) * (h @ w2.astype(f32) + b2.astype(f32)))
    h = jnp.sin(f3.astype(f32) * (h @ w3.astype(f32) + b3.astype(f32)))
    h = h @ w4.astype(f32)
    max_decay = math.log(HYENA_DECAY_TARGET) / HYENA_FAST_PCT
    min_decay = math.log(HYENA_DECAY_TARGET) / HYENA_SLOW_PCT
    deltas = jnp.abs(jnp.linspace(min_decay, max_decay, HYENA_W, dtype=f32))
    window = jnp.exp(-t01[:, None] * deltas[None, :])
    return h * jnp.tile(window, (1, 2))


def _long_conv_bidir(u, filt):
    L, C = u.shape[1], u.shape[2]
    hf, hb = filt[:, :C], filt[:, C:]
    taps = jnp.concatenate([hf, jnp.zeros((1, C), filt.dtype), hb[:0:-1]], axis=0)
    U = jnp.fft.rfft(u.astype(jnp.float32), n=2 * L, axis=1)
    K = jnp.fft.rfft(taps, n=2 * L, axis=0)
    y = jnp.fft.irfft(U * K[None], n=2 * L, axis=1)[:, :L]
    return y.astype(u.dtype)


def _mixer(h, s0_f, s0_b, p):
    B, L, _ = h.shape
    proj = h @ p['w_in']
    qkv = jax.nn.silu(_dwconv(proj[..., COL_K:COL_BETA], p['conv_qkv']))
    k = _l2norm(_heads(qkv[..., COL_K:COL_V]))
    v = _heads(qkv[..., COL_V:COL_Q]).astype(jnp.float32)
    q = _l2norm(_heads(qkv[..., COL_Q:COL_BETA])) * (GDN_HEAD_DIM ** -0.5)
    beta, g = _gdn_gates(proj[..., COL_BETA:COL_Z], p['a_log'], p['dt_bias'])
    o_f, s_f = _gdn_direction(q, k, v, beta[:, :, 0], g[:, :, 0], s0_f)
    o_b, s_b = _gdn_direction(jnp.flip(q, 1), jnp.flip(k, 1), jnp.flip(v, 1),
                              jnp.flip(beta[:, :, 1], 1), jnp.flip(g[:, :, 1], 1), s0_b)
    z = jax.nn.silu(_heads(proj[..., COL_Z:COL_HY]).astype(jnp.float32))
    o = _rmsnorm(o_f + jnp.flip(o_b, 1), p['gdn_norm']) * z
    o = o.reshape(B, L, GDN_W).astype(h.dtype)
    hy = _dwconv(proj[..., COL_HY:], p['conv_hy']) + p['conv_hy_b']
    x0, x1, vh = jnp.split(hy, 3, axis=-1)
    uh = x1 * vh
    filt = _hyena_filter(L, p['filt_w1'], p['filt_b1'], p['filt_freq1'], p['filt_w2'], p['filt_b2'],
                         p['filt_freq2'], p['filt_w3'], p['filt_b3'], p['filt_freq3'], p['filt_w4'])
    y = x0 * (_long_conv_bidir(uh, filt) + p['hyena_bias'] * uh)
    out = jnp.concatenate([o, y], axis=-1) @ p['w_out']
    return out, s_f, s_b


def _block(s, mod, s0_f, s0_b, p):
    sh_a, sc_a, g_a, sh_f, sc_f, g_f = mod
    h = _rmsnorm(s, p['norm_pre_mix']) * (1.0 + sc_a) + sh_a
    out, s_f, s_b = _mixer(h, s0_f, s0_b, p)
    s = s + g_a * _rmsnorm(out, p['norm_post_mix'])
    h = _rmsnorm(s, p['norm_pre_ffn']) * (1.0 + sc_f) + sh_f
    ff = (jax.nn.silu(h @ p['w_gate']) * (h @ p['w_up'])) @ p['w_down']
    s = s + g_f * _rmsnorm(ff, p['norm_post_ffn'])
    return s, s_f, s_b


def _context_states(ctx, mod, p):
    sh_a, sc_a = mod[0], mod[1]
    h = _rmsnorm(ctx, p['norm_pre_mix']) * (1.0 + sc_a) + sh_a
    kv = jax.nn.silu(_dwconv(h @ p['w_in'][:, COL_K:COL_Q], p['conv_qkv'][:, COL_K:COL_Q]))
    k = _l2norm(_heads(kv[..., :GDN_W]))
    v = _heads(kv[..., GDN_W:]).astype(jnp.float32)
    beta, g = _gdn_gates(h @ p['w_in'][:, COL_BETA:COL_Z], p['a_log'], p['dt_bias'])
    s0 = jnp.zeros((ctx.shape[0], GDN_HEADS, GDN_HEAD_DIM, GDN_HEAD_DIM), jnp.float32)
    s_f = _gdn_final_state(k, v, beta[:, :, 0], g[:, :, 0], s0)
    s_b = _gdn_final_state(jnp.flip(k, 1), jnp.flip(v, 1), jnp.flip(beta[:, :, 1], 1), jnp.flip(g[:, :, 1], 1), s0)
    return s_f, s_b


def setup_inputs(seed: int = 0) -> dict:
    key = jax.random.key(seed)
    ks = jax.random.split(key, 34)
    f32 = jnp.float32
    D, NL = D_MODEL, DEPTH

    def nrm(k, shape, scale):
        return jax.random.normal(k, shape, f32) * scale

    dt = jnp.exp(jax.random.uniform(ks[13], (NL, 2, GDN_HEADS), f32, math.log(1e-3), math.log(1e-1)))
    return {
        'x': nrm(ks[0], (BATCH, SEQ, D), 1.0),
        'c': nrm(ks[1], (BATCH, D), 1.0),
        'ctx': nrm(ks[2], (BATCH, CTX_LEN, D), 1.0),
        'c_ctx': nrm(ks[3], (D,), 1.0),
        'w_mod': nrm(ks[4], (NL, D, 6 * D), 0.5 * D ** -0.5),
        'b_mod': nrm(ks[5], (NL, 6 * D), 0.02),
        'norm_pre_mix': 1.0 + nrm(ks[6], (NL, D), 0.05),
        'norm_post_mix': 1.0 + nrm(ks[7], (NL, D), 0.05),
        'norm_pre_ffn': 1.0 + nrm(ks[8], (NL, D), 0.05),
        'norm_post_ffn': 1.0 + nrm(ks[9], (NL, D), 0.05),
        'w_in': nrm(ks[10], (NL, D, IN_COLS), D ** -0.5),
        'conv_qkv': nrm(ks[11], (NL, GDN_CONV, 3 * GDN_W), GDN_CONV ** -0.5),
        'a_log': jnp.log(jax.random.uniform(ks[12], (NL, 2, GDN_HEADS), f32, 1.0, 16.0)),
        'dt_bias': dt + jnp.log(-jnp.expm1(-dt)),
        'gdn_norm': 1.0 + nrm(ks[14], (NL, GDN_HEAD_DIM), 0.05),
        'conv_hy': nrm(ks[15], (NL, HYENA_CONV, 3 * HYENA_W), HYENA_CONV ** -0.5),
        'conv_hy_b': nrm(ks[16], (NL, 3 * HYENA_W), 0.02),
        'filt_w1': nrm(ks[17], (NL, HYENA_EMB, HYENA_FILTER_W), HYENA_EMB ** -0.5),
        'filt_b1': nrm(ks[18], (NL, HYENA_FILTER_W), 0.02),
        'filt_freq1': 1.0 + nrm(ks[19], (NL, HYENA_FILTER_W), 0.05),
        'filt_w2': nrm(ks[20], (NL, HYENA_FILTER_W, HYENA_FILTER_W), HYENA_FILTER_W ** -0.5),
        'filt_b2': nrm(ks[21], (NL, HYENA_FILTER_W), 0.02),
        'filt_freq2': 1.0 + nrm(ks[22], (NL, HYENA_FILTER_W), 0.05),
        'filt_w3': nrm(ks[23], (NL, HYENA_FILTER_W, HYENA_FILTER_W), HYENA_FILTER_W ** -0.5),
        'filt_b3': nrm(ks[24], (NL, HYENA_FILTER_W), 0.02),
        'filt_freq3': 1.0 + nrm(ks[25], (NL, HYENA_FILTER_W), 0.05),
        'filt_w4': nrm(ks[26], (NL, HYENA_FILTER_W, 2 * HYENA_W), 0.02),
        'hyena_bias': nrm(ks[27], (NL, HYENA_W), 1.0),
        'w_out': nrm(ks[28], (NL, MIX_W, D), MIX_W ** -0.5),
        'w_gate': nrm(ks[29], (NL, D, D_FF), D ** -0.5),
        'w_up': nrm(ks[30], (NL, D, D_FF), D ** -0.5),
        'w_down': nrm(ks[31], (NL, D_FF, D), D_FF ** -0.5),
    }


def reference(x, c, ctx, c_ctx, w_mod, b_mod, norm_pre_mix, norm_post_mix, norm_pre_ffn, norm_post_ffn,
              w_in, conv_qkv, a_log, dt_bias, gdn_norm, conv_hy, conv_hy_b,
              filt_w1, filt_b1, filt_freq1, filt_w2, filt_b2, filt_freq2, filt_w3, filt_b3, filt_freq3, filt_w4,
              hyena_bias, w_out, w_gate, w_up, w_down):
    for layer in range(DEPTH):
        p = {
            'norm_pre_mix': norm_pre_mix[layer], 'norm_post_mix': norm_post_mix[layer],
            'norm_pre_ffn': norm_pre_ffn[layer], 'norm_post_ffn': norm_post_ffn[layer],
            'w_in': w_in[layer], 'conv_qkv': conv_qkv[layer], 'a_log': a_log[layer], 'dt_bias': dt_bias[layer],
            'gdn_norm': gdn_norm[layer], 'conv_hy': conv_hy[layer], 'conv_hy_b': conv_hy_b[layer],
            'filt_w1': filt_w1[layer], 'filt_b1': filt_b1[layer], 'filt_freq1': filt_freq1[layer],
            'filt_w2': filt_w2[layer], 'filt_b2': filt_b2[layer], 'filt_freq2': filt_freq2[layer],
            'filt_w3': filt_w3[layer], 'filt_b3': filt_b3[layer], 'filt_freq3': filt_freq3[layer],
            'filt_w4': filt_w4[layer], 'hyena_bias': hyena_bias[layer], 'w_out': w_out[layer],
            'w_gate': w_gate[layer], 'w_up': w_up[layer], 'w_down': w_down[layer],
        }
        mod_x = _adaln(c, w_mod[layer], b_mod[layer])
        mod_c = _adaln(c_ctx[None], w_mod[layer], b_mod[layer])
        if layer + 1 < DEPTH:
            s0 = jnp.zeros((ctx.shape[0], GDN_HEADS, GDN_HEAD_DIM, GDN_HEAD_DIM), jnp.float32)
            ctx, s_f, s_b = _block(ctx, mod_c, s0, s0, p)
        else:
            s_f, s_b = _context_states(ctx, mod_c, p)
        x, _, _ = _block(x, mod_x, s_f, s_b, p)
    return x
```

```python
import functools
import math

import numpy as np
import jax
import jax.numpy as jnp
from jax import lax
from jax.experimental import pallas as pl
from jax.experimental.pallas import tpu as pltpu

F32 = jnp.float32
BF16 = jnp.bfloat16

EPS = 1e-6
GDN_HEADS = 8
GDN_HEAD_DIM = 128
GDN_W = GDN_HEADS * GDN_HEAD_DIM
GDN_CONV = 5
GDN_CHUNK = 64
INV_BLOCK = 16
HYENA_CONV = 3
HYENA_EMB = 33
HYENA_DECAY_TARGET = 1e-2
HYENA_FAST_PCT = 0.3
HYENA_SLOW_PCT = 1.5

LANES = 128
BF16_ROWS = 16
DFT_N2 = 128
DFT_PITCH = 136
DFT_KBLK = 8
V7X_SCOPED_VMEM_CAP = 60000 * 1024


def _cparams(sem, vmem_bytes):
    limit = int(min(max(vmem_bytes, 16 * 1024 * 1024), V7X_SCOPED_VMEM_CAP))
    return pltpu.CompilerParams(dimension_semantics=sem, vmem_limit_bytes=limit)


def _silu(x):
    return x * jax.nn.sigmoid(x)


def _softplus(x):
    return jnp.maximum(x, 0.0) + jnp.log1p(jnp.exp(-jnp.abs(x)))


def _rms_rows(x, w):
    return x * lax.rsqrt(jnp.mean(x * x, axis=-1, keepdims=True) + EPS) * w


def _dot(a, b):
    return jnp.dot(a, b, preferred_element_type=F32)


def _dot_nt(a, b):
    return lax.dot_general(a, b, (((1,), (1,)), ((), ())), preferred_element_type=F32)


def _dot_tn(a, b):
    return lax.dot_general(a, b, (((0,), (0,)), ((), ())), preferred_element_type=F32)


def _dot_hi(a, b):
    return jnp.dot(a, b, preferred_element_type=F32, precision=lax.Precision.HIGHEST)


def _mod_kernel(c_ref, w_ref, b_ref, o_ref):
    a = _silu(c_ref[...]).astype(BF16)
    o_ref[...] = _dot(a, w_ref[...].astype(BF16)) + b_ref[...]


def _adaln_mod(cvec, w_mod, b_mod):
    rows, d = cvec.shape
    n = w_mod.shape[1]
    tn = 1024
    return pl.pallas_call(
        _mod_kernel,
        grid=(n // tn,),
        in_specs=[pl.BlockSpec((rows, d), lambda j: (0, 0)),
                  pl.BlockSpec((d, tn), lambda j: (0, j)),
                  pl.BlockSpec((1, tn), lambda j: (0, j))],
        out_specs=pl.BlockSpec((rows, tn), lambda j: (0, j)),
        out_shape=jax.ShapeDtypeStruct((rows, n), F32),
        compiler_params=_cparams(("arbitrary",), 2 * d * tn * 4 + 4 * d * tn),
        name="adaln_mod",
    )(cvec, w_mod, b_mod.reshape(1, n))


def _inproj_kernel(x_ref, sh_ref, sc_ref, nw_ref, w_ref, wg_ref, o_ref, og_ref, h_ref):
    @pl.when(pl.program_id(2) == 0)
    def _():
        h = _rms_rows(x_ref[0], nw_ref[...]) * (1.0 + sc_ref[0]) + sh_ref[0]
        hb = h.astype(BF16)
        h_ref[...] = hb
        og_ref[0] = _dot(hb, wg_ref[...])

    o_ref[0] = _dot(h_ref[...], w_ref[...]).astype(o_ref.dtype)


def _in_proj(x, sh, sc, norm_w, w_main, w_gate, tm):
    b, l, d = x.shape
    n = w_main.shape[1]
    tn = 1024
    vmem = 2 * tm * d * 4 + 2 * d * tn * 2 + 2 * tm * tn * 2 + tm * d * 2 + 4 * d * LANES * 2 + 4 * tm * d * 4
    return pl.pallas_call(
        _inproj_kernel,
        grid=(b, l // tm, n // tn),
        in_specs=[pl.BlockSpec((1, tm, d), lambda bi, i, j: (bi, i, 0)),
                  pl.BlockSpec((1, 1, d), lambda bi, i, j: (bi, 0, 0)),
                  pl.BlockSpec((1, 1, d), lambda bi, i, j: (bi, 0, 0)),
                  pl.BlockSpec((1, d), lambda bi, i, j: (0, 0)),
                  pl.BlockSpec((d, tn), lambda bi, i, j: (0, j)),
                  pl.BlockSpec((d, LANES), lambda bi, i, j: (0, 0))],
        out_specs=[pl.BlockSpec((1, tm, tn), lambda bi, i, j: (bi, i, j)),
                   pl.BlockSpec((1, tm, LANES), lambda bi, i, j: (bi, i, 0))],
        out_shape=[jax.ShapeDtypeStruct((b, l, n), BF16),
                   jax.ShapeDtypeStruct((b, l, LANES), F32)],
        scratch_shapes=[pltpu.VMEM((tm, d), BF16)],
        compiler_params=_cparams(("arbitrary", "arbitrary", "arbitrary"), vmem),
        name="in_proj",
    )(x, sh, sc, norm_w.reshape(1, d), w_main, w_gate)


def _gdn_prep_kernel(p_ref, pp_ref, pn_ref, ba_ref, cw_ref, gp_ref,
                     k_ref, v_ref, q_ref, gc_ref, gr_ref):
    i = pl.program_id(1)
    ts = p_ref.shape[1]
    has_prev = (i > 0).astype(F32)
    has_next = (i < pl.num_programs(1) - 1).astype(F32)
    pad = GDN_CONV // 2
    outs = (k_ref, v_ref, q_ref)
    for t in range(3 * GDN_HEADS):
        cs = t * LANES
        top = pp_ref[0, :, cs:cs + LANES].astype(F32)[BF16_ROWS - 8:] * has_prev
        mid = p_ref[0, :, cs:cs + LANES].astype(F32)
        bot = pn_ref[0, :, cs:cs + LANES].astype(F32)[:8] * has_next
        ext = jnp.concatenate([top, mid, bot], axis=0)
        acc = ext[8 - pad:8 - pad + ts] * cw_ref[0:1, cs:cs + LANES]
        for d in range(1, GDN_CONV):
            acc = acc + ext[8 - pad + d:8 - pad + d + ts] * cw_ref[d:d + 1, cs:cs + LANES]
        y = _silu(acc)
        grp = t // GDN_HEADS
        hs = (t % GDN_HEADS) * LANES
        if grp == 0:
            y = y * lax.rsqrt(jnp.sum(y * y, axis=-1, keepdims=True) + EPS)
        elif grp == 2:
            y = y * (lax.rsqrt(jnp.sum(y * y, axis=-1, keepdims=True) + EPS) * (GDN_HEAD_DIM ** -0.5))
        outs[grp][0, :, hs:hs + LANES] = y.astype(outs[grp].dtype)

    ba = ba_ref[0]
    lane = lax.broadcasted_iota(jnp.int32, ba.shape, 1)
    beta = jax.nn.sigmoid(ba)
    g = -gp_ref[0:1, :] * _softplus(ba + gp_ref[1:2, :])
    g = jnp.where((lane >= 2 * GDN_HEADS) & (lane < 4 * GDN_HEADS), g, 0.0)
    r = lax.broadcasted_iota(jnp.int32, (GDN_CHUNK, GDN_CHUNK), 0)
    c = lax.broadcasted_iota(jnp.int32, (GDN_CHUNK, GDN_CHUNK), 1)
    lower = (r >= c).astype(F32)
    upper = (r <= c).astype(F32)
    lane_c = lax.broadcasted_iota(jnp.int32, (GDN_CHUNK, LANES), 1)
    for ci in range(ts // GDN_CHUNK):
        rows = slice(ci * GDN_CHUNK, (ci + 1) * GDN_CHUNK)
        gch = g[rows]
        cf = _dot_hi(lower, gch)
        cb = _dot_hi(upper, gch)
        packed = jnp.where(lane_c < 2 * GDN_HEADS, beta[rows],
                           jnp.where(lane_c < 3 * GDN_HEADS, cf, cb))
        gc_ref[0, rows, :] = packed
        gr_ref[0, ci] = packed.T[0:4 * GDN_HEADS, :]


def _gdn_prep(proj, gates, conv_qkv, a_log, dt_bias, ts):
    b, l, _ = proj.shape
    w3 = 3 * GDN_W
    hb = BF16_ROWS
    nblk = l // ts
    cw = jnp.zeros((8, w3), F32).at[:GDN_CONV].set(conv_qkv.astype(F32))
    gp = jnp.zeros((8, LANES), F32)
    gp = gp.at[0, 2 * GDN_HEADS:4 * GDN_HEADS].set(jnp.exp(a_log.astype(F32)).reshape(-1))
    gp = gp.at[1, 2 * GDN_HEADS:4 * GDN_HEADS].set(dt_bias.astype(F32).reshape(-1))
    last_hb = l // hb - 1
    out_kvq = jax.ShapeDtypeStruct((b, l, GDN_W), BF16)
    return pl.pallas_call(
        _gdn_prep_kernel,
        grid=(b, nblk),
        in_specs=[pl.BlockSpec((1, ts, w3), lambda bi, i: (bi, i, 0)),
                  pl.BlockSpec((1, hb, w3), lambda bi, i: (bi, jnp.maximum(i * (ts // hb) - 1, 0), 0)),
                  pl.BlockSpec((1, hb, w3), lambda bi, i: (bi, jnp.minimum((i + 1) * (ts // hb), last_hb), 0)),
                  pl.BlockSpec((1, ts, LANES), lambda bi, i: (bi, i, 0)),
                  pl.BlockSpec((8, w3), lambda bi, i: (0, 0)),
                  pl.BlockSpec((8, LANES), lambda bi, i: (0, 0))],
        out_specs=[pl.BlockSpec((1, ts, GDN_W), lambda bi, i: (bi, i, 0)),
                   pl.BlockSpec((1, ts, GDN_W), lambda bi, i: (bi, i, 0)),
                   pl.BlockSpec((1, ts, GDN_W), lambda bi, i: (bi, i, 0)),
                   pl.BlockSpec((1, ts, LANES), lambda bi, i: (bi, i, 0)),
                   pl.BlockSpec((1, ts // GDN_CHUNK, 4 * GDN_HEADS, GDN_CHUNK), lambda bi, i: (bi, i, 0, 0))],
        out_shape=[out_kvq, out_kvq, out_kvq,
                   jax.ShapeDtypeStruct((b, l, LANES), F32),
                   jax.ShapeDtypeStruct((b, l // GDN_CHUNK, 4 * GDN_HEADS, GDN_CHUNK), F32)],
        compiler_params=_cparams(("arbitrary", "arbitrary"), 32 * 1024 * 1024),
        name="gdn_prep",
    )(proj, proj, proj, gates, cw, gp)


def _tri_inverse(a, blockdiag):
    n = a.shape[0]
    r = lax.broadcasted_iota(jnp.int32, (n, n), 0)
    c = lax.broadcasted_iota(jnp.int32, (n, n), 1)
    eye = (r == c).astype(F32)
    d = jnp.where(blockdiag, a, 0.0)
    e = a - d
    db = d.astype(BF16)
    d2 = _dot(db, db)
    d2b = d2.astype(BF16)
    d4 = _dot(d2b, d2b)
    d4b = d4.astype(BF16)
    d8 = _dot(d4b, d4b)
    x = eye - d
    x = x + _dot(x.astype(BF16), d2b)
    x = x + _dot(x.astype(BF16), d4b)
    tdiag = x + _dot(x.astype(BF16), d8.astype(BF16))
    tdb = tdiag.astype(BF16)
    nn = _dot(tdb, e.astype(BF16))
    nb = nn.astype(BF16)
    n2 = _dot(nb, nb)
    y = tdiag + _dot(n2.astype(BF16), tdb)
    return y - _dot(nb, y.astype(BF16))


def _gdn_chunk(bwd, k_ref, v_ref, q_ref, gc_ref, gr_ref, o_ref, s_ref, sidx):
    n = GDN_CHUNK
    r = lax.broadcasted_iota(jnp.int32, (n, n), 0)
    c = lax.broadcasted_iota(jnp.int32, (n, n), 1)
    if bwd:
        strict, incl = r < c, r <= c
    else:
        strict, incl = r > c, r >= c
    blockdiag = (r // INV_BLOCK) == (c // INV_BLOCK)
    gcol = gc_ref[0]
    grow = gr_ref[0, 0]
    off = GDN_HEADS if bwd else 0
    last = 0 if bwd else n - 1
    for h in range(GDN_HEADS):
        hs = slice(h * GDN_HEAD_DIM, (h + 1) * GDN_HEAD_DIM)
        kb = k_ref[0, :, hs]
        vb = v_ref[0, :, hs]
        qb = q_ref[0, :, hs]
        beta_c = gcol[:, off + h:off + h + 1]
        gc_c = gcol[:, 2 * GDN_HEADS + off + h:2 * GDN_HEADS + off + h + 1]
        beta_r = grow[off + h:off + h + 1, :]
        gc_r = grow[2 * GDN_HEADS + off + h:2 * GDN_HEADS + off + h + 1, :]
        g_last = gc_r[:, last:last + 1]
        dec = jnp.exp(jnp.where(incl, gc_c - gc_r, 0.0))
        kk = _dot_nt(kb, kb)
        qk = _dot_nt(qb, kb)
        a = jnp.where(strict, beta_c * kk * dec, 0.0)
        attn = jnp.where(incl, qk * dec, 0.0)
        t = _tri_inverse(a, blockdiag)
        tb = t * beta_r
        u = _dot(tb.astype(BF16), vb)
        w = _dot((tb * jnp.exp(gc_r)).astype(BF16), kb)
        s = s_ref[sidx, h]
        sb = s.astype(BF16)
        v_new = u - _dot(w.astype(BF16), sb)
        vnb = v_new.astype(BF16)
        o = jnp.exp(gc_c) * _dot(qb, sb) + _dot(attn.astype(BF16), vnb)
        o_ref[0, :, hs] = o.astype(o_ref.dtype)
        kd = (kb.astype(F32) * jnp.exp(g_last - gc_c)).astype(BF16)
        s_ref[sidx, h] = s * jnp.exp(g_last) + _dot_tn(kd, vnb)


def _gdn_scan_kernel(kf, vf, qf, gcf, grf, kb, vb, qb, gcb, grb, s0_ref,
                     of_ref, ob_ref, sfin_ref, s_ref):
    i = pl.program_id(1)

    @pl.when(i == 0)
    def _():
        s_ref[...] = s0_ref[:, 0]

    _gdn_chunk(False, kf, vf, qf, gcf, grf, of_ref, s_ref, 0)
    _gdn_chunk(True, kb, vb, qb, gcb, grb, ob_ref, s_ref, 1)

    @pl.when(i == pl.num_programs(1) - 1)
    def _():
        sfin_ref[:, 0] = s_ref[...]


def _gdn_scan(k, v, q, gcol, grow, s0):
    b, l, _ = k.shape
    nc = l // GDN_CHUNK
    fwd = lambda bi, i: (bi, i, 0)
    bwd = lambda bi, i: (bi, nc - 1 - i, 0)
    fwd4 = lambda bi, i: (bi, i, 0, 0)
    bwd4 = lambda bi, i: (bi, nc - 1 - i, 0, 0)
    blk = (1, GDN_CHUNK, GDN_W)
    gblk = (1, GDN_CHUNK, LANES)
    rblk = (1, 1, 4 * GDN_HEADS, GDN_CHUNK)
    sblk = (2, 1, GDN_HEADS, GDN_HEAD_DIM, GDN_HEAD_DIM)
    smap = lambda bi, i: (0, bi, 0, 0, 0)
    return pl.pallas_call(
        _gdn_scan_kernel,
        grid=(b, nc),
        in_specs=[pl.BlockSpec(blk, fwd), pl.BlockSpec(blk, fwd), pl.BlockSpec(blk, fwd),
                  pl.BlockSpec(gblk, fwd), pl.BlockSpec(rblk, fwd4),
                  pl.BlockSpec(blk, bwd), pl.BlockSpec(blk, bwd), pl.BlockSpec(blk, bwd),
                  pl.BlockSpec(gblk, bwd), pl.BlockSpec(rblk, bwd4),
                  pl.BlockSpec(sblk, smap)],
        out_specs=[pl.BlockSpec(blk, fwd), pl.BlockSpec(blk, bwd), pl.BlockSpec(sblk, smap)],
        out_shape=[jax.ShapeDtypeStruct((b, l, GDN_W), BF16),
                   jax.ShapeDtypeStruct((b, l, GDN_W), BF16),
                   jax.ShapeDtypeStruct((2, b, GDN_HEADS, GDN_HEAD_DIM, GDN_HEAD_DIM), F32)],
        scratch_shapes=[pltpu.VMEM((2, GDN_HEADS, GDN_HEAD_DIM, GDN_HEAD_DIM), F32)],
        compiler_params=_cparams(("arbitrary", "arbitrary"), 32 * 1024 * 1024),
        name="gdn_scan",
    )(k, v, q, gcol, grow, k, v, q, gcol, grow, s0)


def _hyena_mlp_kernel(fr_ref, w1_ref, b1_ref, f1_ref, w2_ref, b2_ref, f2_ref, w3_ref, b3_ref, f3_ref,
                      o_ref, *, seq_len):
    tl = o_ref.shape[0]
    i = pl.program_id(0)
    row = (lax.broadcasted_iota(jnp.int32, (tl, LANES), 0) + i * tl).astype(F32)
    lane = lax.broadcasted_iota(jnp.int32, (tl, LANES), 1)
    bands = (HYENA_EMB - 1) // 2
    t01 = row / max(seq_len - 1, 1)
    ang = (2.0 * math.pi / seq_len) * row * fr_ref[...]
    z = jnp.where(lane == 0, t01,
                  jnp.where(lane <= bands, jnp.cos(ang),
                            jnp.where(lane <= 2 * bands, -jnp.sin(ang), 0.0)))
    h = jnp.sin(f1_ref[...] * (_dot_hi(z, w1_ref[...]) + b1_ref[...]))
    h = jnp.sin(f2_ref[...] * (_dot_hi(h, w2_ref[...]) + b2_ref[...]))
    h = jnp.sin(f3_ref[...] * (_dot_hi(h, w3_ref[...]) + b3_ref[...]))
    o_ref[...] = h


def _hyena_mlp(seq_len, w1, b1, f1, w2, b2, f2, w3, b3, f3):
    fw = w2.shape[0]
    bands = (HYENA_EMB - 1) // 2
    freqs = np.linspace(1e-4, bands - 1, bands, dtype=np.float32)
    fr = np.zeros((1, LANES), np.float32)
    fr[0, 1:1 + bands] = freqs
    fr[0, 1 + bands:1 + 2 * bands] = freqs
    w1p = jnp.zeros((LANES, fw), F32).at[:HYENA_EMB].set(w1.astype(F32))
    tl = min(seq_len, 1024)
    full = lambda i: (0, 0)
    vec = pl.BlockSpec((1, fw), full)
    mat = pl.BlockSpec((fw, fw), full)
    r2 = lambda a: a.astype(F32).reshape(1, fw)
    return pl.pallas_call(
        functools.partial(_hyena_mlp_kernel, seq_len=seq_len),
        grid=(seq_len // tl,),
        in_specs=[pl.BlockSpec((1, LANES), full), pl.BlockSpec((LANES, fw), full), vec, vec,
                  mat, vec, vec, mat, vec, vec],
        out_specs=pl.BlockSpec((tl, fw), lambda i: (i, 0)),
        out_shape=jax.ShapeDtypeStruct((seq_len, fw), F32),
        compiler_params=_cparams(("arbitrary",), 16 * 1024 * 1024),
        name="hyena_mlp",
    )(jnp.asarray(fr), w1p, r2(b1), r2(f1), w2.astype(F32), r2(b2), r2(f2), w3.astype(F32), r2(b3), r2(f3))


def _dft_tables(seq_len):
    n1 = 2 * seq_len // DFT_N2
    n = n1 * DFT_N2
    half = n1 // 2
    k1 = np.arange(n1, dtype=np.float64)[:, None]
    a1 = 2.0 * np.pi * k1 * np.arange(n1, dtype=np.float64)[None, :] / n1
    c1, s1 = np.cos(a1), np.sin(a1)
    f1c = np.block([[c1[:, :half], s1[:, :half]], [-s1[:, :half], c1[:, :half]]])
    f1r = np.concatenate([c1[:, :half], -s1[:, :half]], axis=0)
    ct, st = c1.T[:half], s1.T[:half]
    f1i = np.block([[ct, -st], [st, ct]]) / n
    a2 = 2.0 * np.pi * np.arange(DFT_N2, dtype=np.float64)[:, None] * np.arange(DFT_N2, dtype=np.float64)[None, :] / DFT_N2
    atw = 2.0 * np.pi * k1 * np.arange(DFT_N2, dtype=np.float64)[None, :] / n
    as32 = lambda a: jnp.asarray(a.astype(np.float32))
    return dict(n1=n1, f1c=as32(f1c), f1r=as32(f1r), f1i=as32(f1i),
                f2r=as32(np.cos(a2)), f2i=as32(-np.sin(a2)),
                twr=as32(np.cos(atw)), twi=as32(-np.sin(atw)))


def _twiddled_dft(f2r, f2i, twr_row, twi_row):
    return f2r * twr_row - f2i * twi_row, f2r * twi_row + f2i * twr_row


def _hyena_spec_kernel(h_ref, w4_ref, dl_ref, f1r_ref, f2r_ref, f2i_ref, twr_ref, twi_ref,
                       o_ref, taps_ref, a_ref, *, seq_len, kblk):
    n1 = twr_ref.shape[0]
    half = n1 // 2
    phase = pl.program_id(1)
    kb = pl.program_id(2)

    @pl.when(kb == 0)
    def _():
        lag0 = (phase == 0).astype(F32)

        def taps_block(blk, carry):
            r0 = pl.multiple_of(blk * DFT_N2, DFT_N2)
            t = (lax.broadcasted_iota(jnp.int32, (DFT_N2, LANES), 0) + r0).astype(F32)
            win = jnp.exp(-(t / max(seq_len - 1, 1)) * dl_ref[...])
            keep = jnp.where(t == 0.0, lag0, 1.0)
            d0 = pl.multiple_of(blk * DFT_PITCH, 8)
            taps_ref[pl.ds(d0, DFT_N2), :] = _dot_hi(h_ref[pl.ds(r0, DFT_N2), :], w4_ref[...]) * win * keep
            return carry

        lax.fori_loop(0, half, taps_block, 0)
        f1 = f1r_ref[...]

        def stage1(n2, carry):
            a = _dot_hi(f1, taps_ref[pl.ds(n2, half, stride=DFT_PITCH), :])
            a_ref[0, pl.ds(n2, n1, stride=DFT_PITCH), :] = a[:n1]
            a_ref[1, pl.ds(n2, n1, stride=DFT_PITCH), :] = a[n1:]
            return carry

        lax.fori_loop(0, DFT_N2, stage1, 0)

    f2r = f2r_ref[...]
    f2i = f2i_ref[...]

    def stage2(j, carry):
        k1 = kb * kblk + j
        gr, gi = _twiddled_dft(f2r, f2i, twr_ref[pl.ds(k1, 1), :], twi_ref[pl.ds(k1, 1), :])
        r0 = pl.multiple_of(k1 * DFT_PITCH, 8)
        ar = a_ref[0, pl.ds(r0, DFT_N2), :]
        ai = a_ref[1, pl.ds(r0, DFT_N2), :]
        o_ref[0, 0, j, :, 0:LANES] = _dot_hi(gr, ar) - _dot_hi(gi, ai)
        o_ref[0, 0, j, :, LANES:2 * LANES] = _dot_hi(gr, ai) + _dot_hi(gi, ar)
        return carry

    lax.fori_loop(0, kblk, stage2, 0)


def _hyena_spec(hmlp, w4, tabs, seq_len):
    n1 = tabs["n1"]
    half = n1 // 2
    fw = hmlp.shape[1]
    cw = w4.shape[1] // 2
    ct = cw // LANES
    kblk = min(DFT_KBLK, n1)
    max_decay = math.log(HYENA_DECAY_TARGET) / HYENA_FAST_PCT
    min_decay = math.log(HYENA_DECAY_TARGET) / HYENA_SLOW_PCT
    deltas = jnp.asarray(np.abs(np.linspace(min_decay, max_decay, cw, dtype=np.float32)).reshape(1, cw))
    full = lambda c, p, kb: (0, 0)
    sq = pl.BlockSpec((DFT_N2, DFT_N2), full)
    tw = pl.BlockSpec((n1, DFT_N2), full)
    vmem = (2 * seq_len * LANES * 4 + half * DFT_PITCH * LANES * 4 + 2 * n1 * DFT_PITCH * LANES * 4
            + 2 * kblk * DFT_N2 * 2 * LANES * 4 + 8 * 1024 * 1024)
    return pl.pallas_call(
        functools.partial(_hyena_spec_kernel, seq_len=seq_len, kblk=kblk),
        grid=(ct, 2, n1 // kblk),
        in_specs=[pl.BlockSpec((seq_len, fw), full),
                  pl.BlockSpec((fw, LANES), lambda c, p, kb: (0, p * ct + c)),
                  pl.BlockSpec((1, LANES), lambda c, p, kb: (0, c)),
                  pl.BlockSpec((2 * n1, half), full), sq, sq, tw, tw],
        out_specs=pl.BlockSpec((1, 1, kblk, DFT_N2, 2 * LANES), lambda c, p, kb: (c, p, kb, 0, 0)),
        out_shape=jax.ShapeDtypeStruct((ct, 2, n1, DFT_N2, 2 * LANES), F32),
        scratch_shapes=[pltpu.VMEM((half * DFT_PITCH, LANES), F32),
                        pltpu.VMEM((2, n1 * DFT_PITCH, LANES), F32)],
        compiler_params=_cparams(("arbitrary", "arbitrary", "arbitrary"), vmem),
        name="hyena_spec",
    )(hmlp, w4.astype(F32), deltas, tabs["f1r"], tabs["f2r"], tabs["f2i"], tabs["twr"], tabs["twi"])


def _conv3_block(ref, bi, r0, nrows, seq_len, w_ref, b_ref):
    lo = pl.multiple_of(jnp.maximum(r0 - BF16_ROWS, 0), BF16_ROWS)
    hi = pl.multiple_of(jnp.minimum(r0 + nrows, seq_len - BF16_ROWS), BF16_ROWS)
    top = ref[bi, pl.ds(lo, BF16_ROWS), :].astype(F32)[BF16_ROWS - 8:] * (r0 > 0).astype(F32)
    mid = ref[bi, pl.ds(pl.multiple_of(r0, BF16_ROWS), nrows), :].astype(F32)
    bot = ref[bi, pl.ds(hi, BF16_ROWS), :].astype(F32)[:8] * (r0 + nrows < seq_len).astype(F32)
    ext = jnp.concatenate([top, mid, bot], axis=0)
    return (ext[7:7 + nrows] * w_ref[0:1, :] + ext[8:8 + nrows] * w_ref[1:2, :]
            + ext[9:9 + nrows] * w_ref[2:3, :] + b_ref[...])


def _hyena_conv_kernel(x0_ref, x1_ref, xv_ref, kf_ref, w0_ref, w1_ref, wv_ref, b0_ref, b1_ref, bv_ref, hb_ref,
                       f1c_ref, f1i_ref, f2r_ref, f2i_ref, twr_ref, twi_ref,
                       o_ref, z_ref, a_ref, *, seq_len, kblk):
    n1 = twr_ref.shape[0]
    half = n1 // 2
    kb = pl.program_id(2)
    nkb = pl.num_programs(2)

    def uh_block(bi, r0):
        return (_conv3_block(x1_ref, bi, r0, DFT_N2, seq_len, w1_ref, b1_ref)
                * _conv3_block(xv_ref, bi, r0, DFT_N2, seq_len, wv_ref, bv_ref))

    @pl.when(kb == 0)
    def _():
        def fill(blk, carry):
            r0 = blk * DFT_N2
            d0 = pl.multiple_of(blk * DFT_PITCH, 8)
            z_ref[0, pl.ds(d0, DFT_N2), :] = uh_block(0, r0)
            z_ref[1, pl.ds(d0, DFT_N2), :] = uh_block(1, r0)
            return carry

        lax.fori_loop(0, half, fill, 0)
        f1 = f1c_ref[...].astype(BF16)

        def stage1(n2, carry):
            ur = z_ref[0, pl.ds(n2, half, stride=DFT_PITCH), :]
            ui = z_ref[1, pl.ds(n2, half, stride=DFT_PITCH), :]
            a = _dot(f1, jnp.concatenate([ur, ui], axis=0).astype(BF16))
            a_ref[0, pl.ds(n2, n1, stride=DFT_PITCH), :] = a[:n1]
            a_ref[1, pl.ds(n2, n1, stride=DFT_PITCH), :] = a[n1:]
            return carry

        lax.fori_loop(0, DFT_N2, stage1, 0)

    f2r = f2r_ref[...]
    f2i = f2i_ref[...]

    def stage2(j, carry):
        k1 = kb * kblk + j
        gr, gi = _twiddled_dft(f2r, f2i, twr_ref[pl.ds(k1, 1), :], twi_ref[pl.ds(k1, 1), :])
        grb, gib = gr.astype(BF16), gi.astype(BF16)
        r0 = pl.multiple_of(k1 * DFT_PITCH, 8)
        ar = a_ref[0, pl.ds(r0, DFT_N2), :].astype(BF16)
        ai = a_ref[1, pl.ds(r0, DFT_N2), :].astype(BF16)
        xr = _dot(grb, ar) - _dot(gib, ai)
        xi = _dot(grb, ai) + _dot(gib, ar)
        kr = kf_ref[0, 0, j, :, 0:LANES] + kf_ref[0, 1, j, :, 0:LANES]
        ki = kf_ref[0, 0, j, :, LANES:2 * LANES] - kf_ref[0, 1, j, :, LANES:2 * LANES]
        pr = (xr * kr - xi * ki).astype(BF16)
        pi = (xr * ki + xi * kr).astype(BF16)
        a_ref[0, pl.ds(r0, DFT_N2), :] = _dot_tn(grb, pr) + _dot_tn(gib, pi)
        a_ref[1, pl.ds(r0, DFT_N2), :] = _dot_tn(grb, pi) - _dot_tn(gib, pr)
        return carry

    lax.fori_loop(0, kblk, stage2, 0)

    @pl.when(kb == nkb - 1)
    def _():
        f1 = f1i_ref[...].astype(BF16)

        def stage_last(n2, carry):
            br = a_ref[0, pl.ds(n2, n1, stride=DFT_PITCH), :]
            bi = a_ref[1, pl.ds(n2, n1, stride=DFT_PITCH), :]
            y = _dot(f1, jnp.concatenate([br, bi], axis=0).astype(BF16))
            z_ref[0, pl.ds(n2, half, stride=DFT_PITCH), :] = y[:half]
            z_ref[1, pl.ds(n2, half, stride=DFT_PITCH), :] = y[half:]
            return carry

        lax.fori_loop(0, DFT_N2, stage_last, 0)

        def emit(blk, carry):
            r0 = blk * DFT_N2
            d0 = pl.multiple_of(blk * DFT_PITCH, 8)
            for bi in range(2):
                x0 = _conv3_block(x0_ref, bi, r0, DFT_N2, seq_len, w0_ref, b0_ref)
                y = x0 * (z_ref[bi, pl.ds(d0, DFT_N2), :] + hb_ref[...] * uh_block(bi, r0))
                o_ref[bi, pl.ds(pl.multiple_of(r0, DFT_N2), DFT_N2), :] = y.astype(o_ref.dtype)
            return carry

        lax.fori_loop(0, half, emit, 0)


def _hyena_conv(proj, col0, kf, conv_hy, conv_hy_b, hyena_bias, tabs, seq_len):
    b, l, _ = proj.shape
    n1 = tabs["n1"]
    half = n1 // 2
    cw = conv_hy.shape[1] // 3
    ct = cw // LANES
    kblk = min(DFT_KBLK, n1)
    cb0 = col0 // LANES
    cwp = jnp.zeros((8, 3 * cw), F32).at[:HYENA_CONV].set(conv_hy.astype(F32))
    cbp = conv_hy_b.astype(F32).reshape(1, 3 * cw)
    hbp = hyena_bias.astype(F32).reshape(1, cw)
    full = lambda c, p, kb: (0, 0)

    def xspec(g):
        return pl.BlockSpec((2, l, LANES), lambda c, p, kb: (p, 0, cb0 + g * ct + c),
                            pipeline_mode=pl.Buffered(1))

    def wspec(g):
        return pl.BlockSpec((8, LANES), lambda c, p, kb: (0, g * ct + c))

    def bspec(g):
        return pl.BlockSpec((1, LANES), lambda c, p, kb: (0, g * ct + c))

    sq = pl.BlockSpec((DFT_N2, DFT_N2), full)
    tw = pl.BlockSpec((n1, DFT_N2), full)
    vmem = (3 * 2 * l * LANES * 2 + 2 * 2 * l * LANES * 2 + 2 * half * DFT_PITCH * LANES * 4
            + 2 * n1 * DFT_PITCH * LANES * 4 + 2 * 2 * kblk * DFT_N2 * 2 * LANES * 4 + 6 * 1024 * 1024)
    return pl.pallas_call(
        functools.partial(_hyena_conv_kernel, seq_len=seq_len, kblk=kblk),
        grid=(ct, b // 2, n1 // kblk),
        in_specs=[xspec(0), xspec(1), xspec(2),
                  pl.BlockSpec((1, 2, kblk, DFT_N2, 2 * LANES), lambda c, p, kb: (c, 0, kb, 0, 0)),
                  wspec(0), wspec(1), wspec(2), bspec(0), bspec(1), bspec(2),
                  pl.BlockSpec((1, LANES), lambda c, p, kb: (0, c)),
                  pl.BlockSpec((2 * n1, n1), full), pl.BlockSpec((n1, 2 * n1), full),
                  sq, sq, tw, tw],
        out_specs=pl.BlockSpec((2, l, LANES), lambda c, p, kb: (p, 0, c)),
        out_shape=jax.ShapeDtypeStruct((b, l, cw), BF16),
        scratch_shapes=[pltpu.VMEM((2, half * DFT_PITCH, LANES), F32),
                        pltpu.VMEM((2, n1 * DFT_PITCH, LANES), F32)],
        compiler_params=_cparams(("arbitrary", "arbitrary", "arbitrary"), vmem),
        name="hyena_conv",
    )(proj, proj, proj, kf, cwp, cwp, cwp, cbp, cbp, cbp, hbp,
      tabs["f1c"], tabs["f1i"], tabs["f2r"], tabs["f2i"], tabs["twr"], tabs["twi"])


def _outproj_kernel(of_ref, ob_ref, z_ref, y_ref, x_ref, ga_ref, gn_ref, nw_ref, w_ref, o_ref):
    parts = []
    for h in range(GDN_HEADS):
        hs = slice(h * GDN_HEAD_DIM, (h + 1) * GDN_HEAD_DIM)
        o = of_ref[0, :, hs].astype(F32) + ob_ref[0, :, hs].astype(F32)
        o = _rms_rows(o, gn_ref[...]) * _silu(z_ref[0, :, hs].astype(F32))
        parts.append(o.astype(BF16))
    parts.append(y_ref[0])
    cat = jnp.concatenate(parts, axis=-1)
    out = _dot(cat, w_ref[...])
    o_ref[0] = x_ref[0] + ga_ref[0] * _rms_rows(out, nw_ref[...])


def _out_proj(o_f, o_b, proj, zcol0, y, x, g_a, gdn_norm, norm_w, w_out, tm):
    b, l, d = x.shape
    zb = zcol0 // GDN_W
    row = lambda bi, i: (bi, i, 0)
    vmem = 4 * 2 * tm * GDN_W * 2 + 4 * tm * d * 4 + 2 * (2 * GDN_W) * d * 2 + 6 * tm * d * 4
    return pl.pallas_call(
        _outproj_kernel,
        grid=(b, l // tm),
        in_specs=[pl.BlockSpec((1, tm, GDN_W), row), pl.BlockSpec((1, tm, GDN_W), row),
                  pl.BlockSpec((1, tm, GDN_W), lambda bi, i: (bi, i, zb)),
                  pl.BlockSpec((1, tm, y.shape[2]), row),
                  pl.BlockSpec((1, tm, d), row),
                  pl.BlockSpec((1, 1, d), lambda bi, i: (bi, 0, 0)),
                  pl.BlockSpec((1, GDN_HEAD_DIM), lambda bi, i: (0, 0)),
                  pl.BlockSpec((1, d), lambda bi, i: (0, 0)),
                  pl.BlockSpec(w_out.shape, lambda bi, i: (0, 0))],
        out_specs=pl.BlockSpec((1, tm, d), row),
        out_shape=jax.ShapeDtypeStruct((b, l, d), F32),
        compiler_params=_cparams(("arbitrary", "arbitrary"), vmem),
        name="out_proj",
    )(o_f, o_b, proj, y, x, g_a, gdn_norm.reshape(1, GDN_HEAD_DIM), norm_w.reshape(1, d), w_out)


def _ffn_kernel(s_ref, sh_ref, sc_ref, gf_ref, npre_ref, npost_ref, wg_ref, wu_ref, wd_ref, o_ref, h_ref, acc_ref):
    j = pl.program_id(2)

    @pl.when(j == 0)
    def _():
        h = _rms_rows(s_ref[0], npre_ref[...]) * (1.0 + sc_ref[0]) + sh_ref[0]
        h_ref[...] = h.astype(BF16)
        acc_ref[...] = jnp.zeros_like(acc_ref)

    hb = h_ref[...]
    act = (_silu(_dot(hb, wg_ref[...])) * _dot(hb, wu_ref[...])).astype(BF16)
    acc_ref[...] += _dot(act, wd_ref[...])

    @pl.when(j == pl.num_programs(2) - 1)
    def _():
        o_ref[0] = s_ref[0] + gf_ref[0] * _rms_rows(acc_ref[...], npost_ref[...])


def _ffn(s, sh, sc, g_f, norm_pre, norm_post, w_gate, w_up, w_down, tm, tf):
    b, l, d = s.shape
    f = w_gate.shape[1]
    row = lambda bi, i, j: (bi, i, 0)
    mod = pl.BlockSpec((1, 1, d), lambda bi, i, j: (bi, 0, 0))
    nrm = pl.BlockSpec((1, d), lambda bi, i, j: (0, 0))
    vmem = 4 * tm * d * 4 + tm * d * 4 + tm * d * 2 + 3 * 2 * d * tf * 2 + 4 * tm * tf * 4 + 2 * tm * d * 4
    return pl.pallas_call(
        _ffn_kernel,
        grid=(b, l // tm, f // tf),
        in_specs=[pl.BlockSpec((1, tm, d), row), mod, mod, mod, nrm, nrm,
                  pl.BlockSpec((d, tf), lambda bi, i, j: (0, j)),
                  pl.BlockSpec((d, tf), lambda bi, i, j: (0, j)),
                  pl.BlockSpec((tf, d), lambda bi, i, j: (j, 0))],
        out_specs=pl.BlockSpec((1, tm, d), row),
        out_shape=jax.ShapeDtypeStruct((b, l, d), F32),
        scratch_shapes=[pltpu.VMEM((tm, d), BF16), pltpu.VMEM((tm, d), F32)],
        compiler_params=_cparams(("arbitrary", "arbitrary", "arbitrary"), vmem),
        name="ffn",
    )(s, sh, sc, g_f, norm_pre.reshape(1, d), norm_post.reshape(1, d), w_gate, w_up, w_down)


COL_Z = 3 * GDN_W
COL_HY = 4 * GDN_W


def _split_w_in(w_in):
    d = w_in.shape[0]
    g0 = 3 * GDN_W
    g1 = g0 + 4 * GDN_HEADS
    w_main = jnp.concatenate([w_in[:, :g0], w_in[:, g1:]], axis=1).astype(BF16)
    w_gate = jnp.zeros((d, LANES), BF16).at[:, :4 * GDN_HEADS].set(w_in[:, g0:g1].astype(BF16))
    return w_main, w_gate


def _row_tile(l, want):
    return want if l % want == 0 else l


def _gdn_branch(x, sh, sc, norm_w, w_main, w_gate, conv_qkv, a_log, dt_bias, s0):
    l = x.shape[1]
    proj, gates = _in_proj(x, sh, sc, norm_w, w_main, w_gate, _row_tile(l, 1024))
    k, v, q, gcol, grow = _gdn_prep(proj, gates, conv_qkv, a_log, dt_bias, _row_tile(l, 256))
    o_f, o_b, s_fin = _gdn_scan(k, v, q, gcol, grow, s0)
    return proj, o_f, o_b, s_fin


def kernel(x, c, ctx, c_ctx, w_mod, b_mod, norm_pre_mix, norm_post_mix, norm_pre_ffn, norm_post_ffn, w_in, conv_qkv, a_log, dt_bias, gdn_norm, conv_hy, conv_hy_b, filt_w1, filt_b1, filt_freq1, filt_w2, filt_b2, filt_freq2, filt_w3, filt_b3, filt_freq3, filt_w4, hyena_bias, w_out, w_gate, w_up, w_down):
    depth = w_in.shape[0]
    bsz, seq_len, d = x.shape
    assert depth == 1, "the context stream continuation of deeper stacks is not implemented"
    layer = 0
    cvec = jnp.zeros((8, d), F32).at[:bsz].set(c).at[bsz].set(c_ctx)
    mod = _adaln_mod(cvec, w_mod[layer], b_mod[layer])
    sh_a, sc_a, g_a, sh_f, sc_f, g_f = [m.reshape(8, 1, d) for m in jnp.split(mod, 6, axis=-1)]
    ctx_rows = lambda m: jnp.broadcast_to(m[bsz:bsz + 1], (bsz, 1, d))
    w_main, w_gt = _split_w_in(w_in[layer])
    zeros = jnp.zeros((2, bsz, GDN_HEADS, GDN_HEAD_DIM, GDN_HEAD_DIM), F32)
    _, _, _, s_ctx = _gdn_branch(ctx, ctx_rows(sh_a), ctx_rows(sc_a), norm_pre_mix[layer], w_main, w_gt,
                                 conv_qkv[layer], a_log[layer], dt_bias[layer], zeros)
    proj, o_f, o_b, _ = _gdn_branch(x, sh_a[:bsz], sc_a[:bsz], norm_pre_mix[layer], w_main, w_gt,
                                    conv_qkv[layer], a_log[layer], dt_bias[layer], s_ctx)
    tabs = _dft_tables(seq_len)
    hmlp = _hyena_mlp(seq_len, filt_w1[layer], filt_b1[layer], filt_freq1[layer], filt_w2[layer], filt_b2[layer],
                      filt_freq2[layer], filt_w3[layer], filt_b3[layer], filt_freq3[layer])
    kf = _hyena_spec(hmlp, filt_w4[layer], tabs, seq_len)
    y = _hyena_conv(proj, COL_HY, kf, conv_hy[layer], conv_hy_b[layer], hyena_bias[layer], tabs, seq_len)
    s1 = _out_proj(o_f, o_b, proj, COL_Z, y, x, g_a[:bsz], gdn_norm[layer], norm_post_mix[layer],
                   w_out[layer].astype(BF16), _row_tile(seq_len, 512))
    return _ffn(s1, sh_f[:bsz], sc_f[:bsz], g_f[:bsz], norm_pre_ffn[layer], norm_post_ffn[layer],
                w_gate[layer].astype(BF16), w_up[layer].astype(BF16), w_down[layer].astype(BF16),
                _row_tile(seq_len, 512), 512)
```

```python
import functools
import math

import numpy as np
import jax
import jax.numpy as jnp
from jax import lax
from jax.experimental import pallas as pl
from jax.experimental.pallas import tpu as pltpu

F32 = jnp.float32
BF16 = jnp.bfloat16

EPS = 1e-6
GDN_HEADS = 8
GDN_HEAD_DIM = 128
GDN_W = GDN_HEADS * GDN_HEAD_DIM
GDN_CONV = 5
GDN_CHUNK = 64
INV_BLOCK = 16
HYENA_CONV = 3
HYENA_EMB = 33
HYENA_DECAY_TARGET = 1e-2
HYENA_FAST_PCT = 0.3
HYENA_SLOW_PCT = 1.5

LANES = 128
BF16_ROWS = 16
DFT_N2 = 128
DFT_PITCH = 136
DFT_KBLK = 8
V7X_SCOPED_VMEM_CAP = 60000 * 1024


def _cparams(sem, vmem_bytes):
    limit = int(min(max(vmem_bytes, 16 * 1024 * 1024), V7X_SCOPED_VMEM_CAP))
    return pltpu.CompilerParams(dimension_semantics=sem, vmem_limit_bytes=limit)


def _silu(x):
    return x * jax.nn.sigmoid(x)


def _softplus(x):
    return jnp.maximum(x, 0.0) + jnp.log1p(jnp.exp(-jnp.abs(x)))


def _rms_rows(x, w):
    return x * lax.rsqrt(jnp.mean(x * x, axis=-1, keepdims=True) + EPS) * w


def _dot(a, b):
    return jnp.dot(a, b, preferred_element_type=F32)


def _dot_nt(a, b):
    return lax.dot_general(a, b, (((1,), (1,)), ((), ())), preferred_element_type=F32)


def _dot_tn(a, b):
    return lax.dot_general(a, b, (((0,), (0,)), ((), ())), preferred_element_type=F32)


def _dot_hi(a, b):
    return jnp.dot(a, b, preferred_element_type=F32, precision=lax.Precision.HIGHEST)


def _mod_kernel(c_ref, w_ref, b_ref, o_ref):
    a = _silu(c_ref[...]).astype(BF16)
    o_ref[...] = _dot(a, w_ref[...].astype(BF16)) + b_ref[...]


def _adaln_mod(cvec, w_mod, b_mod):
    rows, d = cvec.shape
    n = w_mod.shape[1]
    tn = 1024
    return pl.pallas_call(
        _mod_kernel,
        grid=(n // tn,),
        in_specs=[pl.BlockSpec((rows, d), lambda j: (0, 0)),
                  pl.BlockSpec((d, tn), lambda j: (0, j)),
                  pl.BlockSpec((1, tn), lambda j: (0, j))],
        out_specs=pl.BlockSpec((rows, tn), lambda j: (0, j)),
        out_shape=jax.ShapeDtypeStruct((rows, n), F32),
        compiler_params=_cparams(("arbitrary",), 2 * d * tn * 4 + 4 * d * tn),
        name="adaln_mod",
    )(cvec, w_mod, b_mod.reshape(1, n))


def _inproj_kernel(x_ref, sh_ref, sc_ref, nw_ref, w_ref, wg_ref, o_ref, og_ref, h_ref):
    @pl.when(pl.program_id(2) == 0)
    def _():
        h = _rms_rows(x_ref[0], nw_ref[...]) * (1.0 + sc_ref[0]) + sh_ref[0]
        hb = h.astype(BF16)
        h_ref[...] = hb
        og_ref[0] = _dot(hb, wg_ref[...])

    o_ref[0] = _dot(h_ref[...], w_ref[...]).astype(o_ref.dtype)


def _in_proj(x, sh, sc, norm_w, w_main, w_gate, tm):
    b, l, d = x.shape
    n = w_main.shape[1]
    tn = 1024
    vmem = 2 * tm * d * 4 + 2 * d * tn * 2 + 2 * tm * tn * 2 + tm * d * 2 + 4 * d * LANES * 2 + 4 * tm * d * 4
    return pl.pallas_call(
        _inproj_kernel,
        grid=(b, l // tm, n // tn),
        in_specs=[pl.BlockSpec((1, tm, d), lambda bi, i, j: (bi, i, 0)),
                  pl.BlockSpec((1, 1, d), lambda bi, i, j: (bi, 0, 0)),
                  pl.BlockSpec((1, 1, d), lambda bi, i, j: (bi, 0, 0)),
                  pl.BlockSpec((1, d), lambda bi, i, j: (0, 0)),
                  pl.BlockSpec((d, tn), lambda bi, i, j: (0, j)),
                  pl.BlockSpec((d, LANES), lambda bi, i, j: (0, 0))],
        out_specs=[pl.BlockSpec((1, tm, tn), lambda bi, i, j: (bi, i, j)),
                   pl.BlockSpec((1, tm, LANES), lambda bi, i, j: (bi, i, 0))],
        out_shape=[jax.ShapeDtypeStruct((b, l, n), BF16),
                   jax.ShapeDtypeStruct((b, l, LANES), F32)],
        scratch_shapes=[pltpu.VMEM((tm, d), BF16)],
        compiler_params=_cparams(("arbitrary", "arbitrary", "arbitrary"), vmem),
        name="in_proj",
    )(x, sh, sc, norm_w.reshape(1, d), w_main, w_gate)


def _gdn_prep_kernel(p_ref, pp_ref, pn_ref, ba_ref, cw_ref, gp_ref,
                     k_ref, v_ref, q_ref, gc_ref, gr_ref):
    i = pl.program_id(1)
    ts = p_ref.shape[1]
    has_prev = (i > 0).astype(F32)
    has_next = (i < pl.num_programs(1) - 1).astype(F32)
    pad = GDN_CONV // 2
    outs = (k_ref, v_ref, q_ref)
    for t in range(3 * GDN_HEADS):
        cs = t * LANES
        top = pp_ref[0, :, cs:cs + LANES].astype(F32)[BF16_ROWS - 8:] * has_prev
        mid = p_ref[0, :, cs:cs + LANES].astype(F32)
        bot = pn_ref[0, :, cs:cs + LANES].astype(F32)[:8] * has_next
        ext = jnp.concatenate([top, mid, bot], axis=0)
        acc = ext[8 - pad:8 - pad + ts] * cw_ref[0:1, cs:cs + LANES]
        for d in range(1, GDN_CONV):
            acc = acc + ext[8 - pad + d:8 - pad + d + ts] * cw_ref[d:d + 1, cs:cs + LANES]
        y = _silu(acc)
        grp = t // GDN_HEADS
        hs = (t % GDN_HEADS) * LANES
        if grp == 0:
            y = y * lax.rsqrt(jnp.sum(y * y, axis=-1, keepdims=True) + EPS)
        elif grp == 2:
            y = y * (lax.rsqrt(jnp.sum(y * y, axis=-1, keepdims=True) + EPS) * (GDN_HEAD_DIM ** -0.5))
        outs[grp][0, :, hs:hs + LANES] = y.astype(outs[grp].dtype)

    ba = ba_ref[0]
    lane = lax.broadcasted_iota(jnp.int32, ba.shape, 1)
    beta = jax.nn.sigmoid(ba)
    g = -gp_ref[0:1, :] * _softplus(ba + gp_ref[1:2, :])
    g = jnp.where((lane >= 2 * GDN_HEADS) & (lane < 4 * GDN_HEADS), g, 0.0)
    r = lax.broadcasted_iota(jnp.int32, (GDN_CHUNK, GDN_CHUNK), 0)
    c = lax.broadcasted_iota(jnp.int32, (GDN_CHUNK, GDN_CHUNK), 1)
    lower = (r >= c).astype(F32)
    upper = (r <= c).astype(F32)
    lane_c = lax.broadcasted_iota(jnp.int32, (GDN_CHUNK, LANES), 1)
    for ci in range(ts // GDN_CHUNK):
        rows = slice(ci * GDN_CHUNK, (ci + 1) * GDN_CHUNK)
        gch = g[rows]
        cf = _dot_hi(lower, gch)
        cb = _dot_hi(upper, gch)
        packed = jnp.where(lane_c < 2 * GDN_HEADS, beta[rows],
                           jnp.where(lane_c < 3 * GDN_HEADS, cf, cb))
        gc_ref[0, rows, :] = packed
        gr_ref[0, ci] = packed.T[0:4 * GDN_HEADS, :]


def _gdn_prep(proj, gates, conv_qkv, a_log, dt_bias, ts):
    b, l, _ = proj.shape
    w3 = 3 * GDN_W
    hb = BF16_ROWS
    nblk = l // ts
    cw = jnp.zeros((8, w3), F32).at[:GDN_CONV].set(conv_qkv.astype(F32))
    gp = jnp.zeros((8, LANES), F32)
    gp = gp.at[0, 2 * GDN_HEADS:4 * GDN_HEADS].set(jnp.exp(a_log.astype(F32)).reshape(-1))
    gp = gp.at[1, 2 * GDN_HEADS:4 * GDN_HEADS].set(dt_bias.astype(F32).reshape(-1))
    last_hb = l // hb - 1
    out_kvq = jax.ShapeDtypeStruct((b, l, GDN_W), BF16)
    return pl.pallas_call(
        _gdn_prep_kernel,
        grid=(b, nblk),
        in_specs=[pl.BlockSpec((1, ts, w3), lambda bi, i: (bi, i, 0)),
                  pl.BlockSpec((1, hb, w3), lambda bi, i: (bi, jnp.maximum(i * (ts // hb) - 1, 0), 0)),
                  pl.BlockSpec((1, hb, w3), lambda bi, i: (bi, jnp.minimum((i + 1) * (ts // hb), last_hb), 0)),
                  pl.BlockSpec((1, ts, LANES), lambda bi, i: (bi, i, 0)),
                  pl.BlockSpec((8, w3), lambda bi, i: (0, 0)),
                  pl.BlockSpec((8, LANES), lambda bi, i: (0, 0))],
        out_specs=[pl.BlockSpec((1, ts, GDN_W), lambda bi, i: (bi, i, 0)),
                   pl.BlockSpec((1, ts, GDN_W), lambda bi, i: (bi, i, 0)),
                   pl.BlockSpec((1, ts, GDN_W), lambda bi, i: (bi, i, 0)),
                   pl.BlockSpec((1, ts, LANES), lambda bi, i: (bi, i, 0)),
                   pl.BlockSpec((1, ts // GDN_CHUNK, 4 * GDN_HEADS, GDN_CHUNK), lambda bi, i: (bi, i, 0, 0))],
        out_shape=[out_kvq, out_kvq, out_kvq,
                   jax.ShapeDtypeStruct((b, l, LANES), F32),
                   jax.ShapeDtypeStruct((b, l // GDN_CHUNK, 4 * GDN_HEADS, GDN_CHUNK), F32)],
        compiler_params=_cparams(("arbitrary", "arbitrary"), 32 * 1024 * 1024),
        name="gdn_prep",
    )(proj, proj, proj, gates, cw, gp)


def _bmm(a, b):
    return jnp.einsum('gmk,gkn->gmn', a, b, preferred_element_type=F32)


def _bmm_nt(a, b):
    return jnp.einsum('gmk,gnk->gmn', a, b, preferred_element_type=F32)


def _bmm_tn(a, b):
    return jnp.einsum('gkm,gkn->gmn', a, b, preferred_element_type=F32)


def _tri_inverse(a, blockdiag):
    n = a.shape[-1]
    r = lax.broadcasted_iota(jnp.int32, (n, n), 0)
    c = lax.broadcasted_iota(jnp.int32, (n, n), 1)
    eye = (r == c).astype(F32)
    d = jnp.where(blockdiag, a, 0.0)
    e = a - d
    db = d.astype(BF16)
    d2b = _bmm(db, db).astype(BF16)
    d4b = _bmm(d2b, d2b).astype(BF16)
    d8b = _bmm(d4b, d4b).astype(BF16)
    x = eye - d
    x = x + _bmm(x.astype(BF16), d2b)
    x = x + _bmm(x.astype(BF16), d4b)
    tdiag = x + _bmm(x.astype(BF16), d8b)
    tdb = tdiag.astype(BF16)
    nb = _bmm(tdb, e.astype(BF16)).astype(BF16)
    n2b = _bmm(nb, nb).astype(BF16)
    y = tdiag + _bmm(n2b, tdb)
    return y - _bmm(nb, y.astype(BF16))


def _gdn_chunk_kernel(k_ref, v_ref, q_ref, gc_ref, gr_ref, u_ref, w_ref, kd_ref, qg_ref, at_ref, eg_ref):
    n = GDN_CHUNK
    nch = k_ref.shape[1] // n
    r = lax.broadcasted_iota(jnp.int32, (n, n), 0)
    c = lax.broadcasted_iota(jnp.int32, (n, n), 1)
    blockdiag = (r // INV_BLOCK) == (c // INV_BLOCK)
    strict = (r > c, r < c)
    incl = (r >= c, r <= c)
    pairs = [(ci, h) for ci in range(nch) for h in range(GDN_HEADS)]
    np_ = len(pairs)

    def tile(ref, ci, h):
        return ref[0, ci * n:(ci + 1) * n, h * GDN_HEAD_DIM:(h + 1) * GDN_HEAD_DIM]

    ks = jnp.stack([tile(k_ref, ci, h) for ci, h in pairs])
    vs = jnp.stack([tile(v_ref, ci, h) for ci, h in pairs])
    qs = jnp.stack([tile(q_ref, ci, h) for ci, h in pairs])
    kk = _bmm_nt(ks, ks)
    qk = _bmm_nt(qs, ks)
    gcol = gc_ref[0]
    a_parts, at_parts, gcb, brow, grow = [], [], [], [], []
    for d in range(2):
        off = d * GDN_HEADS
        cols = lambda lane: jnp.stack([jnp.broadcast_to(gcol[ci * n:(ci + 1) * n, lane + h:lane + h + 1], (n, LANES))
                                       for ci, h in pairs])
        rows = lambda row: jnp.stack([gr_ref[0, ci, row + h:row + h + 1, :] for ci, h in pairs])
        gcb_d = cols(2 * GDN_HEADS + off)
        beta_c = cols(off)[:, :, :n]
        gc_r = rows(2 * GDN_HEADS + off)
        dec = jnp.exp(jnp.where(incl[d], gcb_d[:, :, :n] - gc_r, 0.0))
        a_parts.append(jnp.where(strict[d], beta_c * kk * dec, 0.0))
        at_parts.append(jnp.where(incl[d], qk * dec, 0.0).astype(BF16))
        gcb.append(gcb_d)
        brow.append(rows(off))
        grow.append(gc_r)
    t = _tri_inverse(jnp.concatenate(a_parts, axis=0), blockdiag)
    tb = t * jnp.concatenate(brow, axis=0)
    k2 = jnp.concatenate([ks, ks], axis=0)
    u = _bmm(tb.astype(BF16), jnp.concatenate([vs, vs], axis=0))
    w = _bmm((tb * jnp.exp(jnp.concatenate(grow, axis=0))).astype(BF16), k2)
    kf = ks.astype(F32)
    qf = qs.astype(F32)
    for d in range(2):
        last = 0 if d else n - 1
        g_last = gcb[d][:, last:last + 1, :]
        kd = (kf * jnp.exp(g_last - gcb[d])).astype(BF16)
        qg = (qf * jnp.exp(gcb[d])).astype(BF16)
        eg = jnp.exp(g_last)
        for p, (ci, h) in enumerate(pairs):
            rs = slice(ci * n, (ci + 1) * n)
            hs = slice(h * GDN_HEAD_DIM, (h + 1) * GDN_HEAD_DIM)
            u_ref[d, 0, rs, hs] = u[d * np_ + p]
            w_ref[d, 0, rs, hs] = w[d * np_ + p].astype(BF16)
            kd_ref[d, 0, rs, hs] = kd[p]
            qg_ref[d, 0, rs, hs] = qg[p]
            at_ref[d, 0, h, rs, :] = at_parts[d][p]
            eg_ref[d, 0, ci, h:h + 1, :] = eg[p]


def _gdn_chunk(k, v, q, gcol, grow, ts):
    b, l, _ = k.shape
    nc = l // GDN_CHUNK
    row = lambda bi, i: (bi, i, 0)
    orow = lambda bi, i: (0, bi, i, 0)
    wide = pl.BlockSpec((2, 1, ts, GDN_W), orow)
    shp = lambda dt: jax.ShapeDtypeStruct((2, b, l, GDN_W), dt)
    return pl.pallas_call(
        _gdn_chunk_kernel,
        grid=(b, l // ts),
        in_specs=[pl.BlockSpec((1, ts, GDN_W), row), pl.BlockSpec((1, ts, GDN_W), row),
                  pl.BlockSpec((1, ts, GDN_W), row), pl.BlockSpec((1, ts, LANES), row),
                  pl.BlockSpec((1, ts // GDN_CHUNK, 4 * GDN_HEADS, GDN_CHUNK), lambda bi, i: (bi, i, 0, 0))],
        out_specs=[wide, wide, wide, wide,
                   pl.BlockSpec((2, 1, GDN_HEADS, ts, GDN_CHUNK), lambda bi, i: (0, bi, 0, i, 0)),
                   pl.BlockSpec((2, 1, ts // GDN_CHUNK, GDN_HEADS, LANES), lambda bi, i: (0, bi, i, 0, 0))],
        out_shape=[shp(F32), shp(BF16), shp(BF16), shp(BF16),
                   jax.ShapeDtypeStruct((2, b, GDN_HEADS, l, GDN_CHUNK), BF16),
                   jax.ShapeDtypeStruct((2, b, nc, GDN_HEADS, LANES), F32)],
        compiler_params=_cparams(("arbitrary", "arbitrary"), 32 * 1024 * 1024),
        name="gdn_chunk",
    )(k, v, q, gcol, grow)


def _gdn_scan_kernel(uf, wf, kdf, qgf, atf, egf, ub, wb, kdb, qgb, atb, egb, s0_ref,
                     of_ref, ob_ref, sfin_ref, s_ref):
    i = pl.program_id(1)
    n = GDN_CHUNK
    nch = uf.shape[2] // n
    ng = 2 * GDN_HEADS

    @pl.when(i == 0)
    def _():
        s_ref[...] = s0_ref[:, 0].reshape(ng, GDN_HEAD_DIM, GDN_HEAD_DIM)

    refs = ((uf, wf, kdf, qgf, atf, egf, of_ref), (ub, wb, kdb, qgb, atb, egb, ob_ref))
    for step in range(nch):
        cis = (step, nch - 1 - step)

        def gather(idx, head_major=False):
            out = []
            for d in range(2):
                rs = slice(cis[d] * n, (cis[d] + 1) * n)
                for h in range(GDN_HEADS):
                    if head_major:
                        out.append(refs[d][idx][0, 0, h, rs, :])
                    else:
                        out.append(refs[d][idx][0, 0, rs, h * GDN_HEAD_DIM:(h + 1) * GDN_HEAD_DIM])
            return jnp.stack(out)

        u, w, kd, qg = gather(0), gather(1), gather(2), gather(3)
        at = gather(4, head_major=True)
        eg = jnp.stack([refs[d][5][0, 0, cis[d], h:h + 1, :] for d in range(2) for h in range(GDN_HEADS)])
        s = s_ref[...]
        sb = s.astype(BF16)
        v_new = (u - _bmm(w, sb)).astype(BF16)
        o = _bmm(qg, sb) + _bmm(at, v_new)
        s_ref[...] = s * eg + _bmm_tn(kd, v_new)
        for d in range(2):
            rs = slice(cis[d] * n, (cis[d] + 1) * n)
            for h in range(GDN_HEADS):
                refs[d][6][0, rs, h * GDN_HEAD_DIM:(h + 1) * GDN_HEAD_DIM] = o[d * GDN_HEADS + h].astype(BF16)

    @pl.when(i == pl.num_programs(1) - 1)
    def _():
        sfin_ref[:, 0] = s_ref[...].reshape(2, GDN_HEADS, GDN_HEAD_DIM, GDN_HEAD_DIM)


def _gdn_scan(u, w, kd, qg, at, eg, s0, ts):
    _, b, l, _ = u.shape
    nb = l // ts
    nch = ts // GDN_CHUNK

    def specs(d):
        blk = (lambda i: i) if d == 0 else (lambda i: nb - 1 - i)
        wide = pl.BlockSpec((1, 1, ts, GDN_W), lambda bi, i: (d, bi, blk(i), 0))
        return [wide, wide, wide, wide,
                pl.BlockSpec((1, 1, GDN_HEADS, ts, GDN_CHUNK), lambda bi, i: (d, bi, 0, blk(i), 0)),
                pl.BlockSpec((1, 1, nch, GDN_HEADS, LANES), lambda bi, i: (d, bi, blk(i), 0, 0))]

    sblk = (2, 1, GDN_HEADS, GDN_HEAD_DIM, GDN_HEAD_DIM)
    smap = lambda bi, i: (0, bi, 0, 0, 0)
    oshape = jax.ShapeDtypeStruct((b, l, GDN_W), BF16)
    args = (u, w, kd, qg, at, eg)
    return pl.pallas_call(
        _gdn_scan_kernel,
        grid=(b, nb),
        in_specs=specs(0) + specs(1) + [pl.BlockSpec(sblk, smap)],
        out_specs=[pl.BlockSpec((1, ts, GDN_W), lambda bi, i: (bi, i, 0)),
                   pl.BlockSpec((1, ts, GDN_W), lambda bi, i: (bi, nb - 1 - i, 0)),
                   pl.BlockSpec(sblk, smap)],
        out_shape=[oshape, oshape,
                   jax.ShapeDtypeStruct((2, b, GDN_HEADS, GDN_HEAD_DIM, GDN_HEAD_DIM), F32)],
        scratch_shapes=[pltpu.VMEM((2 * GDN_HEADS, GDN_HEAD_DIM, GDN_HEAD_DIM), F32)],
        compiler_params=_cparams(("arbitrary", "arbitrary"), 32 * 1024 * 1024),
        name="gdn_scan",
    )(*args, *args, s0)


def _hyena_mlp_kernel(fr_ref, w1_ref, b1_ref, f1_ref, w2_ref, b2_ref, f2_ref, w3_ref, b3_ref, f3_ref,
                      o_ref, *, seq_len):
    tl = o_ref.shape[0]
    i = pl.program_id(0)
    row = (lax.broadcasted_iota(jnp.int32, (tl, LANES), 0) + i * tl).astype(F32)
    lane = lax.broadcasted_iota(jnp.int32, (tl, LANES), 1)
    bands = (HYENA_EMB - 1) // 2
    t01 = row / max(seq_len - 1, 1)
    ang = (2.0 * math.pi / seq_len) * row * fr_ref[...]
    z = jnp.where(lane == 0, t01,
                  jnp.where(lane <= bands, jnp.cos(ang),
                            jnp.where(lane <= 2 * bands, -jnp.sin(ang), 0.0)))
    h = jnp.sin(f1_ref[...] * (_dot_hi(z, w1_ref[...]) + b1_ref[...]))
    h = jnp.sin(f2_ref[...] * (_dot_hi(h, w2_ref[...]) + b2_ref[...]))
    h = jnp.sin(f3_ref[...] * (_dot_hi(h, w3_ref[...]) + b3_ref[...]))
    o_ref[...] = h


def _hyena_mlp(seq_len, w1, b1, f1, w2, b2, f2, w3, b3, f3):
    fw = w2.shape[0]
    bands = (HYENA_EMB - 1) // 2
    freqs = np.linspace(1e-4, bands - 1, bands, dtype=np.float32)
    fr = np.zeros((1, LANES), np.float32)
    fr[0, 1:1 + bands] = freqs
    fr[0, 1 + bands:1 + 2 * bands] = freqs
    w1p = jnp.zeros((LANES, fw), F32).at[:HYENA_EMB].set(w1.astype(F32))
    tl = min(seq_len, 1024)
    full = lambda i: (0, 0)
    vec = pl.BlockSpec((1, fw), full)
    mat = pl.BlockSpec((fw, fw), full)
    r2 = lambda a: a.astype(F32).reshape(1, fw)
    return pl.pallas_call(
        functools.partial(_hyena_mlp_kernel, seq_len=seq_len),
        grid=(seq_len // tl,),
        in_specs=[pl.BlockSpec((1, LANES), full), pl.BlockSpec((LANES, fw), full), vec, vec,
                  mat, vec, vec, mat, vec, vec],
        out_specs=pl.BlockSpec((tl, fw), lambda i: (i, 0)),
        out_shape=jax.ShapeDtypeStruct((seq_len, fw), F32),
        compiler_params=_cparams(("arbitrary",), 16 * 1024 * 1024),
        name="hyena_mlp",
    )(jnp.asarray(fr), w1p, r2(b1), r2(f1), w2.astype(F32), r2(b2), r2(f2), w3.astype(F32), r2(b3), r2(f3))


def _dft_tables(seq_len):
    n1 = 2 * seq_len // DFT_N2
    n = n1 * DFT_N2
    half = n1 // 2
    k1 = np.arange(n1, dtype=np.float64)[:, None]
    a1 = 2.0 * np.pi * k1 * np.arange(n1, dtype=np.float64)[None, :] / n1
    c1, s1 = np.cos(a1), np.sin(a1)
    f1c = np.block([[c1[:, :half], s1[:, :half]], [-s1[:, :half], c1[:, :half]]])
    f1r = np.concatenate([c1[:, :half], -s1[:, :half]], axis=0)
    ct, st = c1.T[:half], s1.T[:half]
    f1i = np.block([[ct, -st], [st, ct]]) / n
    a2 = 2.0 * np.pi * np.arange(DFT_N2, dtype=np.float64)[:, None] * np.arange(DFT_N2, dtype=np.float64)[None, :] / DFT_N2
    atw = 2.0 * np.pi * k1 * np.arange(DFT_N2, dtype=np.float64)[None, :] / n
    as32 = lambda a: jnp.asarray(a.astype(np.float32))
    return dict(n1=n1, f1c=as32(f1c), f1r=as32(f1r), f1i=as32(f1i),
                f2r=as32(np.cos(a2)), f2i=as32(-np.sin(a2)),
                twr=as32(np.cos(atw)), twi=as32(-np.sin(atw)))


def _twiddled_dft(f2r, f2i, twr_row, twi_row):
    return f2r * twr_row - f2i * twi_row, f2r * twi_row + f2i * twr_row


def _split_bf16(x):
    hi = x.astype(BF16)
    return hi, (x - hi.astype(F32)).astype(BF16)


def _dot2(a, b):
    ah = a.astype(BF16)
    bh, bl = _split_bf16(b)
    return _dot(ah, bh) + _dot(ah, bl)


def _hyena_spec_kernel(h_ref, w4f_ref, w4b_ref, dl_ref, f1r_ref, f2r_ref, f2i_ref, twr_ref, twi_ref,
                       o_ref, taps_ref, a_ref, *, seq_len, kblk):
    n1 = twr_ref.shape[0]
    half = n1 // 2
    kb = pl.program_id(1)
    q0, q1, q2, q3 = (slice(i * LANES, (i + 1) * LANES) for i in range(4))

    @pl.when(kb == 0)
    def _():
        def taps_block(blk, carry):
            r0 = pl.multiple_of(blk * DFT_N2, DFT_N2)
            t = (lax.broadcasted_iota(jnp.int32, (DFT_N2, LANES), 0) + r0).astype(F32)
            win = jnp.exp(-(t / max(seq_len - 1, 1)) * dl_ref[...])
            d0 = pl.multiple_of(blk * DFT_PITCH, 8)
            hb = h_ref[pl.ds(r0, DFT_N2), :]
            taps_ref[0, pl.ds(d0, DFT_N2), :] = _dot_hi(hb, w4f_ref[...]) * win
            taps_ref[1, pl.ds(d0, DFT_N2), :] = _dot_hi(hb, w4b_ref[...]) * jnp.where(t == 0.0, 0.0, win)
            return carry

        lax.fori_loop(0, half, taps_block, 0)
        f1 = f1r_ref[...]

        def stage1(n2, carry):
            x = jnp.concatenate([taps_ref[0, pl.ds(n2, half, stride=DFT_PITCH), :],
                                 taps_ref[1, pl.ds(n2, half, stride=DFT_PITCH), :]], axis=1)
            a = _dot2(f1, x)
            a_ref[0, pl.ds(n2, n1, stride=DFT_PITCH), :] = a[:n1, q0]
            a_ref[1, pl.ds(n2, n1, stride=DFT_PITCH), :] = a[:n1, q1]
            a_ref[2, pl.ds(n2, n1, stride=DFT_PITCH), :] = a[n1:, q0]
            a_ref[3, pl.ds(n2, n1, stride=DFT_PITCH), :] = a[n1:, q1]
            return carry

        lax.fori_loop(0, DFT_N2, stage1, 0, unroll=2)

    f2r = f2r_ref[...]
    f2i = f2i_ref[...]

    def stage2(j, carry):
        k1 = kb * kblk + j
        gr, gi = _twiddled_dft(f2r, f2i, twr_ref[pl.ds(k1, 1), :], twi_ref[pl.ds(k1, 1), :])
        r0 = pl.multiple_of(k1 * DFT_PITCH, 8)
        acat = jnp.concatenate([a_ref[s, pl.ds(r0, DFT_N2), :] for s in range(4)], axis=1)
        r = _dot2(jnp.concatenate([gr, gi], axis=0), acat)
        top, bot = r[:DFT_N2], r[DFT_N2:]
        ff_r, ff_i = top[:, q0] - bot[:, q2], top[:, q2] + bot[:, q0]
        fb_r, fb_i = top[:, q1] - bot[:, q3], top[:, q3] + bot[:, q1]
        o_ref[0, j, :, q0] = ff_r + fb_r
        o_ref[0, j, :, q1] = ff_i - fb_i
        return carry

    lax.fori_loop(0, kblk, stage2, 0, unroll=2)


def _hyena_spec(hmlp, w4, tabs, seq_len):
    n1 = tabs["n1"]
    half = n1 // 2
    fw = hmlp.shape[1]
    cw = w4.shape[1] // 2
    ct = cw // LANES
    kblk = min(DFT_KBLK, n1)
    max_decay = math.log(HYENA_DECAY_TARGET) / HYENA_FAST_PCT
    min_decay = math.log(HYENA_DECAY_TARGET) / HYENA_SLOW_PCT
    deltas = jnp.asarray(np.abs(np.linspace(min_decay, max_decay, cw, dtype=np.float32)).reshape(1, cw))
    full = lambda c, kb: (0, 0)
    sq = pl.BlockSpec((DFT_N2, DFT_N2), full)
    tw = pl.BlockSpec((n1, DFT_N2), full)
    vmem = (2 * seq_len * LANES * 4 + half * DFT_PITCH * 2 * LANES * 4 + n1 * DFT_PITCH * 4 * LANES * 4
            + 2 * kblk * DFT_N2 * 2 * LANES * 4 + 8 * 1024 * 1024)
    w4 = w4.astype(F32)
    return pl.pallas_call(
        functools.partial(_hyena_spec_kernel, seq_len=seq_len, kblk=kblk),
        grid=(ct, n1 // kblk),
        in_specs=[pl.BlockSpec((seq_len, fw), full),
                  pl.BlockSpec((fw, LANES), lambda c, kb: (0, c)),
                  pl.BlockSpec((fw, LANES), lambda c, kb: (0, ct + c)),
                  pl.BlockSpec((1, LANES), lambda c, kb: (0, c)),
                  pl.BlockSpec((2 * n1, half), full), sq, sq, tw, tw],
        out_specs=pl.BlockSpec((1, kblk, DFT_N2, 2 * LANES), lambda c, kb: (c, kb, 0, 0)),
        out_shape=jax.ShapeDtypeStruct((ct, n1, DFT_N2, 2 * LANES), F32),
        scratch_shapes=[pltpu.VMEM((2, half * DFT_PITCH, LANES), F32),
                        pltpu.VMEM((4, n1 * DFT_PITCH, LANES), F32)],
        compiler_params=_cparams(("arbitrary", "arbitrary"), vmem),
        name="hyena_spec",
    )(hmlp, w4, w4, deltas, tabs["f1r"], tabs["f2r"], tabs["f2i"], tabs["twr"], tabs["twi"])


def _conv3_block(ref, bi, r0, nrows, seq_len, w_ref, b_ref):
    lo = pl.multiple_of(jnp.maximum(r0 - BF16_ROWS, 0), BF16_ROWS)
    hi = pl.multiple_of(jnp.minimum(r0 + nrows, seq_len - BF16_ROWS), BF16_ROWS)
    top = ref[bi, pl.ds(lo, BF16_ROWS), :].astype(F32)[BF16_ROWS - 8:] * (r0 > 0).astype(F32)
    mid = ref[bi, pl.ds(pl.multiple_of(r0, BF16_ROWS), nrows), :].astype(F32)
    bot = ref[bi, pl.ds(hi, BF16_ROWS), :].astype(F32)[:8] * (r0 + nrows < seq_len).astype(F32)
    ext = jnp.concatenate([top, mid, bot], axis=0)
    return (ext[7:7 + nrows] * w_ref[0:1, :] + ext[8:8 + nrows] * w_ref[1:2, :]
            + ext[9:9 + nrows] * w_ref[2:3, :] + b_ref[...])


def _hyena_conv_kernel(x0_ref, x1_ref, xv_ref, kf_ref, w0_ref, w1_ref, wv_ref, b0_ref, b1_ref, bv_ref, hb_ref,
                       f1c_ref, f1i_ref, f2r_ref, f2i_ref, twr_ref, twi_ref,
                       o_ref, z_ref, a_ref, *, seq_len, kblk):
    n1 = twr_ref.shape[0]
    half = n1 // 2
    kb = pl.program_id(2)
    nkb = pl.num_programs(2)

    def uh_block(bi, r0):
        return (_conv3_block(x1_ref, bi, r0, DFT_N2, seq_len, w1_ref, b1_ref)
                * _conv3_block(xv_ref, bi, r0, DFT_N2, seq_len, wv_ref, bv_ref))

    @pl.when(kb == 0)
    def _():
        def fill(blk, carry):
            r0 = blk * DFT_N2
            d0 = pl.multiple_of(blk * DFT_PITCH, 8)
            z_ref[0, pl.ds(d0, DFT_N2), :] = uh_block(0, r0)
            z_ref[1, pl.ds(d0, DFT_N2), :] = uh_block(1, r0)
            return carry

        lax.fori_loop(0, half, fill, 0)
        f1 = f1c_ref[...].astype(BF16)

        def stage1(n2, carry):
            ur = z_ref[0, pl.ds(n2, half, stride=DFT_PITCH), :]
            ui = z_ref[1, pl.ds(n2, half, stride=DFT_PITCH), :]
            a = _dot(f1, jnp.concatenate([ur, ui], axis=0).astype(BF16))
            a_ref[0, pl.ds(n2, n1, stride=DFT_PITCH), :] = a[:n1]
            a_ref[1, pl.ds(n2, n1, stride=DFT_PITCH), :] = a[n1:]
            return carry

        lax.fori_loop(0, DFT_N2, stage1, 0, unroll=4)

    f2r = f2r_ref[...]
    f2i = f2i_ref[...]
    lo, hi = slice(0, DFT_N2), slice(DFT_N2, 2 * DFT_N2)

    def stage2(j, carry):
        k1 = kb * kblk + j
        gr, gi = _twiddled_dft(f2r, f2i, twr_ref[pl.ds(k1, 1), :], twi_ref[pl.ds(k1, 1), :])
        r0 = pl.multiple_of(k1 * DFT_PITCH, 8)
        acat = jnp.concatenate([a_ref[0, pl.ds(r0, DFT_N2), :], a_ref[1, pl.ds(r0, DFT_N2), :]], axis=1)
        r = _dot(jnp.concatenate([gr, gi], axis=0).astype(BF16), acat.astype(BF16))
        xr = r[lo, lo] - r[hi, hi]
        xi = r[lo, hi] + r[hi, lo]
        kr = kf_ref[0, j, :, lo]
        ki = kf_ref[0, j, :, hi]
        pcat = jnp.concatenate([xr * kr - xi * ki, xr * ki + xi * kr], axis=1).astype(BF16)
        q = _dot_tn(jnp.concatenate([gr, gi], axis=1).astype(BF16), pcat)
        a_ref[0, pl.ds(r0, DFT_N2), :] = q[lo, lo] + q[hi, hi]
        a_ref[1, pl.ds(r0, DFT_N2), :] = q[lo, hi] - q[hi, lo]
        return carry

    lax.fori_loop(0, kblk, stage2, 0, unroll=2)

    @pl.when(kb == nkb - 1)
    def _():
        f1 = f1i_ref[...].astype(BF16)

        def stage_last(n2, carry):
            br = a_ref[0, pl.ds(n2, n1, stride=DFT_PITCH), :]
            bi = a_ref[1, pl.ds(n2, n1, stride=DFT_PITCH), :]
            y = _dot(f1, jnp.concatenate([br, bi], axis=0).astype(BF16))
            z_ref[0, pl.ds(n2, half, stride=DFT_PITCH), :] = y[:half]
            z_ref[1, pl.ds(n2, half, stride=DFT_PITCH), :] = y[half:]
            return carry

        lax.fori_loop(0, DFT_N2, stage_last, 0, unroll=4)

        def emit(blk, carry):
            r0 = blk * DFT_N2
            d0 = pl.multiple_of(blk * DFT_PITCH, 8)
            for bi in range(2):
                x0 = _conv3_block(x0_ref, bi, r0, DFT_N2, seq_len, w0_ref, b0_ref)
                y = x0 * (z_ref[bi, pl.ds(d0, DFT_N2), :] + hb_ref[...] * uh_block(bi, r0))
                o_ref[bi, pl.ds(pl.multiple_of(r0, DFT_N2), DFT_N2), :] = y.astype(o_ref.dtype)
            return carry

        lax.fori_loop(0, half, emit, 0)


def _hyena_conv(proj, col0, kf, conv_hy, conv_hy_b, hyena_bias, tabs, seq_len):
    b, l, _ = proj.shape
    n1 = tabs["n1"]
    half = n1 // 2
    cw = conv_hy.shape[1] // 3
    ct = cw // LANES
    kblk = min(DFT_KBLK, n1)
    cb0 = col0 // LANES
    cwp = jnp.zeros((8, 3 * cw), F32).at[:HYENA_CONV].set(conv_hy.astype(F32))
    cbp = conv_hy_b.astype(F32).reshape(1, 3 * cw)
    hbp = hyena_bias.astype(F32).reshape(1, cw)
    full = lambda c, p, kb: (0, 0)

    def xspec(g):
        return pl.BlockSpec((2, l, LANES), lambda c, p, kb: (p, 0, cb0 + g * ct + c),
                            pipeline_mode=pl.Buffered(1))

    def wspec(g):
        return pl.BlockSpec((8, LANES), lambda c, p, kb: (0, g * ct + c))

    def bspec(g):
        return pl.BlockSpec((1, LANES), lambda c, p, kb: (0, g * ct + c))

    sq = pl.BlockSpec((DFT_N2, DFT_N2), full)
    tw = pl.BlockSpec((n1, DFT_N2), full)
    vmem = (3 * 2 * l * LANES * 2 + 2 * 2 * l * LANES * 2 + 2 * half * DFT_PITCH * LANES * 4
            + 2 * n1 * DFT_PITCH * LANES * 4 + 2 * kblk * DFT_N2 * 2 * LANES * 4 + 6 * 1024 * 1024)
    return pl.pallas_call(
        functools.partial(_hyena_conv_kernel, seq_len=seq_len, kblk=kblk),
        grid=(ct, b // 2, n1 // kblk),
        in_specs=[xspec(0), xspec(1), xspec(2),
                  pl.BlockSpec((1, kblk, DFT_N2, 2 * LANES), lambda c, p, kb: (c, kb, 0, 0)),
                  wspec(0), wspec(1), wspec(2), bspec(0), bspec(1), bspec(2),
                  pl.BlockSpec((1, LANES), lambda c, p, kb: (0, c)),
                  pl.BlockSpec((2 * n1, n1), full), pl.BlockSpec((n1, 2 * n1), full),
                  sq, sq, tw, tw],
        out_specs=pl.BlockSpec((2, l, LANES), lambda c, p, kb: (p, 0, c)),
        out_shape=jax.ShapeDtypeStruct((b, l, cw), BF16),
        scratch_shapes=[pltpu.VMEM((2, half * DFT_PITCH, LANES), F32),
                        pltpu.VMEM((2, n1 * DFT_PITCH, LANES), F32)],
        compiler_params=_cparams(("arbitrary", "arbitrary", "arbitrary"), vmem),
        name="hyena_conv",
    )(proj, proj, proj, kf, cwp, cwp, cwp, cbp, cbp, cbp, hbp,
      tabs["f1c"], tabs["f1i"], tabs["f2r"], tabs["f2i"], tabs["twr"], tabs["twi"])


def _outproj_kernel(of_ref, ob_ref, z_ref, y_ref, x_ref, ga_ref, gn_ref, nw_ref, w_ref, o_ref):
    parts = []
    for h in range(GDN_HEADS):
        hs = slice(h * GDN_HEAD_DIM, (h + 1) * GDN_HEAD_DIM)
        o = of_ref[0, :, hs].astype(F32) + ob_ref[0, :, hs].astype(F32)
        o = _rms_rows(o, gn_ref[...]) * _silu(z_ref[0, :, hs].astype(F32))
        parts.append(o.astype(BF16))
    parts.append(y_ref[0])
    cat = jnp.concatenate(parts, axis=-1)
    out = _dot(cat, w_ref[...])
    o_ref[0] = x_ref[0] + ga_ref[0] * _rms_rows(out, nw_ref[...])


def _out_proj(o_f, o_b, proj, zcol0, y, x, g_a, gdn_norm, norm_w, w_out, tm):
    b, l, d = x.shape
    zb = zcol0 // GDN_W
    row = lambda bi, i: (bi, i, 0)
    vmem = 4 * 2 * tm * GDN_W * 2 + 4 * tm * d * 4 + 2 * (2 * GDN_W) * d * 2 + 6 * tm * d * 4
    return pl.pallas_call(
        _outproj_kernel,
        grid=(b, l // tm),
        in_specs=[pl.BlockSpec((1, tm, GDN_W), row), pl.BlockSpec((1, tm, GDN_W), row),
                  pl.BlockSpec((1, tm, GDN_W), lambda bi, i: (bi, i, zb)),
                  pl.BlockSpec((1, tm, y.shape[2]), row),
                  pl.BlockSpec((1, tm, d), row),
                  pl.BlockSpec((1, 1, d), lambda bi, i: (bi, 0, 0)),
                  pl.BlockSpec((1, GDN_HEAD_DIM), lambda bi, i: (0, 0)),
                  pl.BlockSpec((1, d), lambda bi, i: (0, 0)),
                  pl.BlockSpec(w_out.shape, lambda bi, i: (0, 0))],
        out_specs=pl.BlockSpec((1, tm, d), row),
        out_shape=jax.ShapeDtypeStruct((b, l, d), F32),
        compiler_params=_cparams(("arbitrary", "arbitrary"), vmem),
        name="out_proj",
    )(o_f, o_b, proj, y, x, g_a, gdn_norm.reshape(1, GDN_HEAD_DIM), norm_w.reshape(1, d), w_out)


def _ffn_kernel(s_ref, sh_ref, sc_ref, gf_ref, npre_ref, npost_ref, wg_ref, wu_ref, wd_ref, o_ref, h_ref, acc_ref):
    j = pl.program_id(2)

    @pl.when(j == 0)
    def _():
        h = _rms_rows(s_ref[0], npre_ref[...]) * (1.0 + sc_ref[0]) + sh_ref[0]
        h_ref[...] = h.astype(BF16)
        acc_ref[...] = jnp.zeros_like(acc_ref)

    hb = h_ref[...]
    act = (_silu(_dot(hb, wg_ref[...])) * _dot(hb, wu_ref[...])).astype(BF16)
    acc_ref[...] += _dot(act, wd_ref[...])

    @pl.when(j == pl.num_programs(2) - 1)
    def _():
        o_ref[0] = s_ref[0] + gf_ref[0] * _rms_rows(acc_ref[...], npost_ref[...])


def _ffn(s, sh, sc, g_f, norm_pre, norm_post, w_gate, w_up, w_down, tm, tf):
    b, l, d = s.shape
    f = w_gate.shape[1]
    row = lambda bi, i, j: (bi, i, 0)
    mod = pl.BlockSpec((1, 1, d), lambda bi, i, j: (bi, 0, 0))
    nrm = pl.BlockSpec((1, d), lambda bi, i, j: (0, 0))
    vmem = 4 * tm * d * 4 + tm * d * 4 + tm * d * 2 + 3 * 2 * d * tf * 2 + 4 * tm * tf * 4 + 2 * tm * d * 4
    return pl.pallas_call(
        _ffn_kernel,
        grid=(b, l // tm, f // tf),
        in_specs=[pl.BlockSpec((1, tm, d), row), mod, mod, mod, nrm, nrm,
                  pl.BlockSpec((d, tf), lambda bi, i, j: (0, j)),
                  pl.BlockSpec((d, tf), lambda bi, i, j: (0, j)),
                  pl.BlockSpec((tf, d), lambda bi, i, j: (j, 0))],
        out_specs=pl.BlockSpec((1, tm, d), row),
        out_shape=jax.ShapeDtypeStruct((b, l, d), F32),
        scratch_shapes=[pltpu.VMEM((tm, d), BF16), pltpu.VMEM((tm, d), F32)],
        compiler_params=_cparams(("arbitrary", "arbitrary", "arbitrary"), vmem),
        name="ffn",
    )(s, sh, sc, g_f, norm_pre.reshape(1, d), norm_post.reshape(1, d), w_gate, w_up, w_down)


COL_Z = 3 * GDN_W
COL_HY = 4 * GDN_W


def _split_w_in(w_in):
    d = w_in.shape[0]
    g0 = 3 * GDN_W
    g1 = g0 + 4 * GDN_HEADS
    w_main = jnp.concatenate([w_in[:, :g0], w_in[:, g1:]], axis=1).astype(BF16)
    w_gate = jnp.zeros((d, LANES), BF16).at[:, :4 * GDN_HEADS].set(w_in[:, g0:g1].astype(BF16))
    return w_main, w_gate


def _row_tile(l, want):
    return want if l % want == 0 else l


def _gdn_branch(x, sh, sc, norm_w, w_main, w_gate, conv_qkv, a_log, dt_bias, s0):
    l = x.shape[1]
    proj, gates = _in_proj(x, sh, sc, norm_w, w_main, w_gate, _row_tile(l, 1024))
    k, v, q, gcol, grow = _gdn_prep(proj, gates, conv_qkv, a_log, dt_bias, _row_tile(l, 256))
    u, w, kd, qg, at, eg = _gdn_chunk(k, v, q, gcol, grow, _row_tile(l, 2 * GDN_CHUNK))
    o_f, o_b, s_fin = _gdn_scan(u, w, kd, qg, at, eg, s0, _row_tile(l, 4 * GDN_CHUNK))
    return proj, o_f, o_b, s_fin


def kernel(x, c, ctx, c_ctx, w_mod, b_mod, norm_pre_mix, norm_post_mix, norm_pre_ffn, norm_post_ffn, w_in, conv_qkv, a_log, dt_bias, gdn_norm, conv_hy, conv_hy_b, filt_w1, filt_b1, filt_freq1, filt_w2, filt_b2, filt_freq2, filt_w3, filt_b3, filt_freq3, filt_w4, hyena_bias, w_out, w_gate, w_up, w_down):
    depth = w_in.shape[0]
    bsz, seq_len, d = x.shape
    assert depth == 1, "the context stream continuation of deeper stacks is not implemented"
    layer = 0
    cvec = jnp.zeros((8, d), F32).at[:bsz].set(c).at[bsz].set(c_ctx)
    mod = _adaln_mod(cvec, w_mod[layer], b_mod[layer])
    sh_a, sc_a, g_a, sh_f, sc_f, g_f = [m.reshape(8, 1, d) for m in jnp.split(mod, 6, axis=-1)]
    ctx_rows = lambda m: jnp.broadcast_to(m[bsz:bsz + 1], (bsz, 1, d))
    w_main, w_gt = _split_w_in(w_in[layer])
    zeros = jnp.zeros((2, bsz, GDN_HEADS, GDN_HEAD_DIM, GDN_HEAD_DIM), F32)
    _, _, _, s_ctx = _gdn_branch(ctx, ctx_rows(sh_a), ctx_rows(sc_a), norm_pre_mix[layer], w_main, w_gt,
                                 conv_qkv[layer], a_log[layer], dt_bias[layer], zeros)
    proj, o_f, o_b, _ = _gdn_branch(x, sh_a[:bsz], sc_a[:bsz], norm_pre_mix[layer], w_main, w_gt,
                                    conv_qkv[layer], a_log[layer], dt_bias[layer], s_ctx)
    tabs = _dft_tables(seq_len)
    hmlp = _hyena_mlp(seq_len, filt_w1[layer], filt_b1[layer], filt_freq1[layer], filt_w2[layer], filt_b2[layer],
                      filt_freq2[layer], filt_w3[layer], filt_b3[layer], filt_freq3[layer])
    kf = _hyena_spec(hmlp, filt_w4[layer], tabs, seq_len)
    y = _hyena_conv(proj, COL_HY, kf, conv_hy[layer], conv_hy_b[layer], hyena_bias[layer], tabs, seq_len)
    s1 = _out_proj(o_f, o_b, proj, COL_Z, y, x, g_a[:bsz], gdn_norm[layer], norm_post_mix[layer],
                   w_out[layer].astype(BF16), _row_tile(seq_len, 512))
    return _ffn(s1, sh_f[:bsz], sc_f[:bsz], g_f[:bsz], norm_pre_ffn[layer], norm_post_ffn[layer],
                w_gate[layer].astype(BF16), w_up[layer].astype(BF16), w_down[layer].astype(BF16),
                _row_tile(seq_len, 512), 512)
```

```python
import functools
import math

import numpy as np
import jax
import jax.numpy as jnp
from jax import lax
from jax.experimental import pallas as pl
from jax.experimental.pallas import tpu as pltpu

F32 = jnp.float32
BF16 = jnp.bfloat16

EPS = 1e-6
GDN_HEADS = 8
GDN_HEAD_DIM = 128
GDN_W = GDN_HEADS * GDN_HEAD_DIM
GDN_CONV = 5
GDN_CHUNK = 64
INV_BLOCK = 16
HYENA_CONV = 3
HYENA_EMB = 33
HYENA_DECAY_TARGET = 1e-2
HYENA_FAST_PCT = 0.3
HYENA_SLOW_PCT = 1.5

LANES = 128
BF16_ROWS = 16
DFT_N2 = 128
DFT_PITCH = 136
DFT_KBLK = 16
V7X_SCOPED_VMEM_CAP = 60000 * 1024


def _cparams(sem, vmem_bytes):
    limit = int(min(max(vmem_bytes, 16 * 1024 * 1024), V7X_SCOPED_VMEM_CAP))
    return pltpu.CompilerParams(dimension_semantics=sem, vmem_limit_bytes=limit)


def _silu(x):
    return x * jax.nn.sigmoid(x)


def _softplus(x):
    return jnp.maximum(x, 0.0) + jnp.log1p(jnp.exp(-jnp.abs(x)))


def _rms_rows(x, w):
    return x * lax.rsqrt(jnp.mean(x * x, axis=-1, keepdims=True) + EPS) * w


def _dot(a, b):
    return jnp.dot(a, b, preferred_element_type=F32)


def _dot_nt(a, b):
    return lax.dot_general(a, b, (((1,), (1,)), ((), ())), preferred_element_type=F32)


def _dot_tn(a, b):
    return lax.dot_general(a, b, (((0,), (0,)), ((), ())), preferred_element_type=F32)


def _dot_hi(a, b):
    return jnp.dot(a, b, preferred_element_type=F32, precision=lax.Precision.HIGHEST)


def _mod_kernel(c_ref, w_ref, b_ref, o_ref):
    a = _silu(c_ref[...]).astype(BF16)
    o_ref[...] = _dot(a, w_ref[...].astype(BF16)) + b_ref[...]


def _adaln_mod(cvec, w_mod, b_mod):
    rows, d = cvec.shape
    n = w_mod.shape[1]
    tn = 1024
    return pl.pallas_call(
        _mod_kernel,
        grid=(n // tn,),
        in_specs=[pl.BlockSpec((rows, d), lambda j: (0, 0)),
                  pl.BlockSpec((d, tn), lambda j: (0, j)),
                  pl.BlockSpec((1, tn), lambda j: (0, j))],
        out_specs=pl.BlockSpec((rows, tn), lambda j: (0, j)),
        out_shape=jax.ShapeDtypeStruct((rows, n), F32),
        compiler_params=_cparams(("arbitrary",), 2 * d * tn * 4 + 4 * d * tn),
        name="adaln_mod",
    )(cvec, w_mod, b_mod.reshape(1, n))


def _inproj_kernel(x_ref, sh_ref, sc_ref, nw_ref, w_ref, wg_ref, o_ref, og_ref, h_ref):
    @pl.when(pl.program_id(2) == 0)
    def _():
        h = _rms_rows(x_ref[0], nw_ref[...]) * (1.0 + sc_ref[0]) + sh_ref[0]
        hb = h.astype(BF16)
        h_ref[...] = hb
        og_ref[0] = _dot(hb, wg_ref[...])

    o_ref[0] = _dot(h_ref[...], w_ref[...]).astype(o_ref.dtype)


def _in_proj(x, sh, sc, norm_w, w_main, w_gate, tm):
    b, l, d = x.shape
    n = w_main.shape[1]
    tn = 1024
    vmem = 2 * tm * d * 4 + 2 * d * tn * 2 + 2 * tm * tn * 2 + tm * d * 2 + 4 * d * LANES * 2 + 4 * tm * d * 4
    return pl.pallas_call(
        _inproj_kernel,
        grid=(b, l // tm, n // tn),
        in_specs=[pl.BlockSpec((1, tm, d), lambda bi, i, j: (bi, i, 0)),
                  pl.BlockSpec((1, 1, d), lambda bi, i, j: (bi, 0, 0)),
                  pl.BlockSpec((1, 1, d), lambda bi, i, j: (bi, 0, 0)),
                  pl.BlockSpec((1, d), lambda bi, i, j: (0, 0)),
                  pl.BlockSpec((d, tn), lambda bi, i, j: (0, j)),
                  pl.BlockSpec((d, LANES), lambda bi, i, j: (0, 0))],
        out_specs=[pl.BlockSpec((1, tm, tn), lambda bi, i, j: (bi, i, j)),
                   pl.BlockSpec((1, tm, LANES), lambda bi, i, j: (bi, i, 0))],
        out_shape=[jax.ShapeDtypeStruct((b, l, n), BF16),
                   jax.ShapeDtypeStruct((b, l, LANES), F32)],
        scratch_shapes=[pltpu.VMEM((tm, d), BF16)],
        compiler_params=_cparams(("arbitrary", "arbitrary", "arbitrary"), vmem),
        name="in_proj",
    )(x, sh, sc, norm_w.reshape(1, d), w_main, w_gate)


def _gdn_prep_kernel(p_ref, pp_ref, pn_ref, ba_ref, cw_ref, gp_ref,
                     k_ref, v_ref, q_ref, gc_ref, gr_ref):
    i = pl.program_id(1)
    ts = p_ref.shape[1]
    has_prev = (i > 0).astype(F32)
    has_next = (i < pl.num_programs(1) - 1).astype(F32)
    pad = GDN_CONV // 2
    outs = (k_ref, v_ref, q_ref)
    for t in range(3 * GDN_HEADS):
        cs = t * LANES
        top = pp_ref[0, :, cs:cs + LANES].astype(F32)[BF16_ROWS - 8:] * has_prev
        mid = p_ref[0, :, cs:cs + LANES].astype(F32)
        bot = pn_ref[0, :, cs:cs + LANES].astype(F32)[:8] * has_next
        ext = jnp.concatenate([top, mid, bot], axis=0)
        acc = ext[8 - pad:8 - pad + ts] * cw_ref[0:1, cs:cs + LANES]
        for d in range(1, GDN_CONV):
            acc = acc + ext[8 - pad + d:8 - pad + d + ts] * cw_ref[d:d + 1, cs:cs + LANES]
        y = _silu(acc)
        grp = t // GDN_HEADS
        hs = (t % GDN_HEADS) * LANES
        if grp == 0:
            y = y * lax.rsqrt(jnp.sum(y * y, axis=-1, keepdims=True) + EPS)
        elif grp == 2:
            y = y * (lax.rsqrt(jnp.sum(y * y, axis=-1, keepdims=True) + EPS) * (GDN_HEAD_DIM ** -0.5))
        outs[grp][0, :, hs:hs + LANES] = y.astype(outs[grp].dtype)

    ba = ba_ref[0]
    lane = lax.broadcasted_iota(jnp.int32, ba.shape, 1)
    beta = jax.nn.sigmoid(ba)
    g = -gp_ref[0:1, :] * _softplus(ba + gp_ref[1:2, :])
    g = jnp.where((lane >= 2 * GDN_HEADS) & (lane < 4 * GDN_HEADS), g, 0.0)
    r = lax.broadcasted_iota(jnp.int32, (GDN_CHUNK, GDN_CHUNK), 0)
    c = lax.broadcasted_iota(jnp.int32, (GDN_CHUNK, GDN_CHUNK), 1)
    lower = (r >= c).astype(F32)
    upper = (r <= c).astype(F32)
    lane_c = lax.broadcasted_iota(jnp.int32, (GDN_CHUNK, LANES), 1)
    for ci in range(ts // GDN_CHUNK):
        rows = slice(ci * GDN_CHUNK, (ci + 1) * GDN_CHUNK)
        gch = g[rows]
        cf = _dot_hi(lower, gch)
        cb = _dot_hi(upper, gch)
        packed = jnp.where(lane_c < 2 * GDN_HEADS, beta[rows],
                           jnp.where(lane_c < 3 * GDN_HEADS, cf, cb))
        gc_ref[0, rows, :] = packed
        gr_ref[0, ci] = packed.T[0:4 * GDN_HEADS, :]


def _gdn_prep(proj, gates, conv_qkv, a_log, dt_bias, ts):
    b, l, _ = proj.shape
    w3 = 3 * GDN_W
    hb = BF16_ROWS
    nblk = l // ts
    cw = jnp.zeros((8, w3), F32).at[:GDN_CONV].set(conv_qkv.astype(F32))
    gp = jnp.zeros((8, LANES), F32)
    gp = gp.at[0, 2 * GDN_HEADS:4 * GDN_HEADS].set(jnp.exp(a_log.astype(F32)).reshape(-1))
    gp = gp.at[1, 2 * GDN_HEADS:4 * GDN_HEADS].set(dt_bias.astype(F32).reshape(-1))
    last_hb = l // hb - 1
    out_kvq = jax.ShapeDtypeStruct((b, l, GDN_W), BF16)
    return pl.pallas_call(
        _gdn_prep_kernel,
        grid=(b, nblk),
        in_specs=[pl.BlockSpec((1, ts, w3), lambda bi, i: (bi, i, 0)),
                  pl.BlockSpec((1, hb, w3), lambda bi, i: (bi, jnp.maximum(i * (ts // hb) - 1, 0), 0)),
                  pl.BlockSpec((1, hb, w3), lambda bi, i: (bi, jnp.minimum((i + 1) * (ts // hb), last_hb), 0)),
                  pl.BlockSpec((1, ts, LANES), lambda bi, i: (bi, i, 0)),
                  pl.BlockSpec((8, w3), lambda bi, i: (0, 0)),
                  pl.BlockSpec((8, LANES), lambda bi, i: (0, 0))],
        out_specs=[pl.BlockSpec((1, ts, GDN_W), lambda bi, i: (bi, i, 0)),
                   pl.BlockSpec((1, ts, GDN_W), lambda bi, i: (bi, i, 0)),
                   pl.BlockSpec((1, ts, GDN_W), lambda bi, i: (bi, i, 0)),
                   pl.BlockSpec((1, ts, LANES), lambda bi, i: (bi, i, 0)),
                   pl.BlockSpec((1, ts // GDN_CHUNK, 4 * GDN_HEADS, GDN_CHUNK), lambda bi, i: (bi, i, 0, 0))],
        out_shape=[out_kvq, out_kvq, out_kvq,
                   jax.ShapeDtypeStruct((b, l, LANES), F32),
                   jax.ShapeDtypeStruct((b, l // GDN_CHUNK, 4 * GDN_HEADS, GDN_CHUNK), F32)],
        compiler_params=_cparams(("arbitrary", "arbitrary"), 32 * 1024 * 1024),
        name="gdn_prep",
    )(proj, proj, proj, gates, cw, gp)


def _bmm(a, b):
    return jnp.einsum('gmk,gkn->gmn', a, b, preferred_element_type=F32)


def _bmm_nt(a, b):
    return jnp.einsum('gmk,gnk->gmn', a, b, preferred_element_type=F32)


def _bmm_tn(a, b):
    return jnp.einsum('gkm,gkn->gmn', a, b, preferred_element_type=F32)


def _tri_inverse(a, blockdiag):
    n = a.shape[-1]
    r = lax.broadcasted_iota(jnp.int32, (n, n), 0)
    c = lax.broadcasted_iota(jnp.int32, (n, n), 1)
    eye = (r == c).astype(F32)
    d = jnp.where(blockdiag, a, 0.0)
    e = a - d
    db = d.astype(BF16)
    d2b = _bmm(db, db).astype(BF16)
    d4b = _bmm(d2b, d2b).astype(BF16)
    d8b = _bmm(d4b, d4b).astype(BF16)
    x = eye - d
    x = x + _bmm(x.astype(BF16), d2b)
    x = x + _bmm(x.astype(BF16), d4b)
    tdiag = x + _bmm(x.astype(BF16), d8b)
    tdb = tdiag.astype(BF16)
    nb = _bmm(tdb, e.astype(BF16)).astype(BF16)
    n2b = _bmm(nb, nb).astype(BF16)
    y = tdiag + _bmm(n2b, tdb)
    return y - _bmm(nb, y.astype(BF16))


def _gdn_chunk_kernel(k_ref, v_ref, q_ref, gc_ref, gr_ref, u_ref, w_ref, kd_ref, qg_ref, at_ref, eg_ref):
    n = GDN_CHUNK
    nch = k_ref.shape[1] // n
    r = lax.broadcasted_iota(jnp.int32, (n, n), 0)
    c = lax.broadcasted_iota(jnp.int32, (n, n), 1)
    blockdiag = (r // INV_BLOCK) == (c // INV_BLOCK)
    strict = (r > c, r < c)
    incl = (r >= c, r <= c)
    pairs = [(ci, h) for ci in range(nch) for h in range(GDN_HEADS)]
    np_ = len(pairs)

    def tile(ref, ci, h):
        return ref[0, ci * n:(ci + 1) * n, h * GDN_HEAD_DIM:(h + 1) * GDN_HEAD_DIM]

    ks = jnp.stack([tile(k_ref, ci, h) for ci, h in pairs])
    vs = jnp.stack([tile(v_ref, ci, h) for ci, h in pairs])
    qs = jnp.stack([tile(q_ref, ci, h) for ci, h in pairs])
    kk = _bmm_nt(ks, ks)
    qk = _bmm_nt(qs, ks)
    gcol = gc_ref[0]
    a_parts, at_parts, gcb, brow, grow = [], [], [], [], []
    for d in range(2):
        off = d * GDN_HEADS
        cols = lambda lane: jnp.stack([jnp.broadcast_to(gcol[ci * n:(ci + 1) * n, lane + h:lane + h + 1], (n, LANES))
                                       for ci, h in pairs])
        rows = lambda row: jnp.stack([gr_ref[0, ci, row + h:row + h + 1, :] for ci, h in pairs])
        gcb_d = cols(2 * GDN_HEADS + off)
        beta_c = cols(off)[:, :, :n]
        gc_r = rows(2 * GDN_HEADS + off)
        dec = jnp.exp(jnp.where(incl[d], gcb_d[:, :, :n] - gc_r, 0.0))
        a_parts.append(jnp.where(strict[d], beta_c * kk * dec, 0.0))
        at_parts.append(jnp.where(incl[d], qk * dec, 0.0).astype(BF16))
        gcb.append(gcb_d)
        brow.append(rows(off))
        grow.append(gc_r)
    t = _tri_inverse(jnp.concatenate(a_parts, axis=0), blockdiag)
    tb = t * jnp.concatenate(brow, axis=0)
    k2 = jnp.concatenate([ks, ks], axis=0)
    u = _bmm(tb.astype(BF16), jnp.concatenate([vs, vs], axis=0))
    w = _bmm((tb * jnp.exp(jnp.concatenate(grow, axis=0))).astype(BF16), k2)
    kf = ks.astype(F32)
    qf = qs.astype(F32)
    for d in range(2):
        last = 0 if d else n - 1
        g_last = gcb[d][:, last:last + 1, :]
        kd = (kf * jnp.exp(g_last - gcb[d])).astype(BF16)
        qg = (qf * jnp.exp(gcb[d])).astype(BF16)
        eg = jnp.exp(g_last)
        for p, (ci, h) in enumerate(pairs):
            rs = slice(ci * n, (ci + 1) * n)
            hs = slice(h * GDN_HEAD_DIM, (h + 1) * GDN_HEAD_DIM)
            u_ref[d, 0, rs, hs] = u[d * np_ + p]
            w_ref[d, 0, rs, hs] = w[d * np_ + p].astype(BF16)
            kd_ref[d, 0, rs, hs] = kd[p]
            qg_ref[d, 0, rs, hs] = qg[p]
            at_ref[d, 0, h, rs, :] = at_parts[d][p]
            eg_ref[d, 0, ci, h:h + 1, :] = eg[p]


def _gdn_chunk(k, v, q, gcol, grow, ts):
    b, l, _ = k.shape
    nc = l // GDN_CHUNK
    row = lambda bi, i: (bi, i, 0)
    orow = lambda bi, i: (0, bi, i, 0)
    wide = pl.BlockSpec((2, 1, ts, GDN_W), orow)
    shp = lambda dt: jax.ShapeDtypeStruct((2, b, l, GDN_W), dt)
    return pl.pallas_call(
        _gdn_chunk_kernel,
        grid=(b, l // ts),
        in_specs=[pl.BlockSpec((1, ts, GDN_W), row), pl.BlockSpec((1, ts, GDN_W), row),
                  pl.BlockSpec((1, ts, GDN_W), row), pl.BlockSpec((1, ts, LANES), row),
                  pl.BlockSpec((1, ts // GDN_CHUNK, 4 * GDN_HEADS, GDN_CHUNK), lambda bi, i: (bi, i, 0, 0))],
        out_specs=[wide, wide, wide, wide,
                   pl.BlockSpec((2, 1, GDN_HEADS, ts, GDN_CHUNK), lambda bi, i: (0, bi, 0, i, 0)),
                   pl.BlockSpec((2, 1, ts // GDN_CHUNK, GDN_HEADS, LANES), lambda bi, i: (0, bi, i, 0, 0))],
        out_shape=[shp(F32), shp(BF16), shp(BF16), shp(BF16),
                   jax.ShapeDtypeStruct((2, b, GDN_HEADS, l, GDN_CHUNK), BF16),
                   jax.ShapeDtypeStruct((2, b, nc, GDN_HEADS, LANES), F32)],
        compiler_params=_cparams(("arbitrary", "arbitrary"), 32 * 1024 * 1024),
        name="gdn_chunk",
    )(k, v, q, gcol, grow)


def _gdn_scan_kernel(uf, wf, kdf, qgf, atf, egf, ub, wb, kdb, qgb, atb, egb, s0_ref,
                     of_ref, ob_ref, sfin_ref, s_ref):
    i = pl.program_id(1)
    n = GDN_CHUNK
    nch = uf.shape[2] // n
    ng = 2 * GDN_HEADS

    @pl.when(i == 0)
    def _():
        s_ref[...] = s0_ref[:, 0].reshape(ng, GDN_HEAD_DIM, GDN_HEAD_DIM)

    refs = ((uf, wf, kdf, qgf, atf, egf, of_ref), (ub, wb, kdb, qgb, atb, egb, ob_ref))
    for step in range(nch):
        cis = (step, nch - 1 - step)

        def gather(idx, head_major=False):
            out = []
            for d in range(2):
                rs = slice(cis[d] * n, (cis[d] + 1) * n)
                for h in range(GDN_HEADS):
                    if head_major:
                        out.append(refs[d][idx][0, 0, h, rs, :])
                    else:
                        out.append(refs[d][idx][0, 0, rs, h * GDN_HEAD_DIM:(h + 1) * GDN_HEAD_DIM])
            return jnp.stack(out)

        u, w, kd, qg = gather(0), gather(1), gather(2), gather(3)
        at = gather(4, head_major=True)
        eg = jnp.stack([refs[d][5][0, 0, cis[d], h:h + 1, :] for d in range(2) for h in range(GDN_HEADS)])
        s = s_ref[...]
        sb = s.astype(BF16)
        v_new = (u - _bmm(w, sb)).astype(BF16)
        o = _bmm(qg, sb) + _bmm(at, v_new)
        s_ref[...] = s * eg + _bmm_tn(kd, v_new)
        for d in range(2):
            rs = slice(cis[d] * n, (cis[d] + 1) * n)
            for h in range(GDN_HEADS):
                refs[d][6][0, rs, h * GDN_HEAD_DIM:(h + 1) * GDN_HEAD_DIM] = o[d * GDN_HEADS + h].astype(BF16)

    @pl.when(i == pl.num_programs(1) - 1)
    def _():
        sfin_ref[:, 0] = s_ref[...].reshape(2, GDN_HEADS, GDN_HEAD_DIM, GDN_HEAD_DIM)


def _gdn_scan(u, w, kd, qg, at, eg, s0, ts):
    _, b, l, _ = u.shape
    nb = l // ts
    nch = ts // GDN_CHUNK

    def specs(d):
        blk = (lambda i: i) if d == 0 else (lambda i: nb - 1 - i)
        wide = pl.BlockSpec((1, 1, ts, GDN_W), lambda bi, i: (d, bi, blk(i), 0))
        return [wide, wide, wide, wide,
                pl.BlockSpec((1, 1, GDN_HEADS, ts, GDN_CHUNK), lambda bi, i: (d, bi, 0, blk(i), 0)),
                pl.BlockSpec((1, 1, nch, GDN_HEADS, LANES), lambda bi, i: (d, bi, blk(i), 0, 0))]

    sblk = (2, 1, GDN_HEADS, GDN_HEAD_DIM, GDN_HEAD_DIM)
    smap = lambda bi, i: (0, bi, 0, 0, 0)
    oshape = jax.ShapeDtypeStruct((b, l, GDN_W), BF16)
    args = (u, w, kd, qg, at, eg)
    return pl.pallas_call(
        _gdn_scan_kernel,
        grid=(b, nb),
        in_specs=specs(0) + specs(1) + [pl.BlockSpec(sblk, smap)],
        out_specs=[pl.BlockSpec((1, ts, GDN_W), lambda bi, i: (bi, i, 0)),
                   pl.BlockSpec((1, ts, GDN_W), lambda bi, i: (bi, nb - 1 - i, 0)),
                   pl.BlockSpec(sblk, smap)],
        out_shape=[oshape, oshape,
                   jax.ShapeDtypeStruct((2, b, GDN_HEADS, GDN_HEAD_DIM, GDN_HEAD_DIM), F32)],
        scratch_shapes=[pltpu.VMEM((2 * GDN_HEADS, GDN_HEAD_DIM, GDN_HEAD_DIM), F32)],
        compiler_params=_cparams(("arbitrary", "arbitrary"), 32 * 1024 * 1024),
        name="gdn_scan",
    )(*args, *args, s0)


def _hyena_mlp_kernel(fr_ref, w1_ref, b1_ref, f1_ref, w2_ref, b2_ref, f2_ref, w3_ref, b3_ref, f3_ref,
                      o_ref, *, seq_len):
    tl = o_ref.shape[0]
    i = pl.program_id(0)
    row = (lax.broadcasted_iota(jnp.int32, (tl, LANES), 0) + i * tl).astype(F32)
    lane = lax.broadcasted_iota(jnp.int32, (tl, LANES), 1)
    bands = (HYENA_EMB - 1) // 2
    t01 = row / max(seq_len - 1, 1)
    ang = (2.0 * math.pi / seq_len) * row * fr_ref[...]
    z = jnp.where(lane == 0, t01,
                  jnp.where(lane <= bands, jnp.cos(ang),
                            jnp.where(lane <= 2 * bands, -jnp.sin(ang), 0.0)))
    h = jnp.sin(f1_ref[...] * (_dot_hi(z, w1_ref[...]) + b1_ref[...]))
    h = jnp.sin(f2_ref[...] * (_dot_hi(h, w2_ref[...]) + b2_ref[...]))
    h = jnp.sin(f3_ref[...] * (_dot_hi(h, w3_ref[...]) + b3_ref[...]))
    o_ref[...] = h


def _hyena_mlp(seq_len, w1, b1, f1, w2, b2, f2, w3, b3, f3):
    fw = w2.shape[0]
    bands = (HYENA_EMB - 1) // 2
    freqs = np.linspace(1e-4, bands - 1, bands, dtype=np.float32)
    fr = np.zeros((1, LANES), np.float32)
    fr[0, 1:1 + bands] = freqs
    fr[0, 1 + bands:1 + 2 * bands] = freqs
    w1p = jnp.zeros((LANES, fw), F32).at[:HYENA_EMB].set(w1.astype(F32))
    tl = min(seq_len, 1024)
    full = lambda i: (0, 0)
    vec = pl.BlockSpec((1, fw), full)
    mat = pl.BlockSpec((fw, fw), full)
    r2 = lambda a: a.astype(F32).reshape(1, fw)
    return pl.pallas_call(
        functools.partial(_hyena_mlp_kernel, seq_len=seq_len),
        grid=(seq_len // tl,),
        in_specs=[pl.BlockSpec((1, LANES), full), pl.BlockSpec((LANES, fw), full), vec, vec,
                  mat, vec, vec, mat, vec, vec],
        out_specs=pl.BlockSpec((tl, fw), lambda i: (i, 0)),
        out_shape=jax.ShapeDtypeStruct((seq_len, fw), F32),
        compiler_params=_cparams(("arbitrary",), 16 * 1024 * 1024),
        name="hyena_mlp",
    )(jnp.asarray(fr), w1p, r2(b1), r2(f1), w2.astype(F32), r2(b2), r2(f2), w3.astype(F32), r2(b3), r2(f3))


def _dft_tables(seq_len):
    n1 = 2 * seq_len // DFT_N2
    n = n1 * DFT_N2
    half = n1 // 2
    k1 = np.arange(n1, dtype=np.float64)[:, None]
    a1 = 2.0 * np.pi * k1 * np.arange(n1, dtype=np.float64)[None, :] / n1
    c1, s1 = np.cos(a1), np.sin(a1)
    f1c = np.block([[c1[:, :half], s1[:, :half]], [-s1[:, :half], c1[:, :half]]])
    f1r = np.concatenate([c1[:, :half], -s1[:, :half]], axis=0)
    ct, st = c1.T[:half], s1.T[:half]
    f1i = np.block([[ct, -st], [st, ct]]) / n
    a2 = 2.0 * np.pi * np.arange(DFT_N2, dtype=np.float64)[:, None] * np.arange(DFT_N2, dtype=np.float64)[None, :] / DFT_N2
    atw = 2.0 * np.pi * k1 * np.arange(DFT_N2, dtype=np.float64)[None, :] / n
    as32 = lambda a: jnp.asarray(a.astype(np.float32))
    return dict(n1=n1, f1c=as32(f1c), f1r=as32(f1r), f1i=as32(f1i),
                f2r=as32(np.cos(a2)), f2i=as32(-np.sin(a2)),
                twr=as32(np.cos(atw)), twi=as32(-np.sin(atw)))


def _twiddled_dft(f2r, f2i, twr_row, twi_row):
    return f2r * twr_row - f2i * twi_row, f2r * twi_row + f2i * twr_row


def _split_bf16(x):
    hi = x.astype(BF16)
    return hi, (x - hi.astype(F32)).astype(BF16)


def _dot2(a, b):
    ah = a.astype(BF16)
    bh, bl = _split_bf16(b)
    return _dot(ah, bh) + _dot(ah, bl)


def _hyena_spec_kernel(h_ref, w4f_ref, w4b_ref, dl_ref, f1r_ref, f2r_ref, f2i_ref, twr_ref, twi_ref,
                       o_ref, taps_ref, a_ref, *, seq_len, kblk):
    n1 = twr_ref.shape[0]
    half = n1 // 2
    kb = pl.program_id(1)
    q0, q1, q2, q3 = (slice(i * LANES, (i + 1) * LANES) for i in range(4))

    @pl.when(kb == 0)
    def _():
        def taps_block(blk, carry):
            r0 = pl.multiple_of(blk * DFT_N2, DFT_N2)
            t = (lax.broadcasted_iota(jnp.int32, (DFT_N2, LANES), 0) + r0).astype(F32)
            win = jnp.exp(-(t / max(seq_len - 1, 1)) * dl_ref[...])
            d0 = pl.multiple_of(blk * DFT_PITCH, 8)
            hb = h_ref[pl.ds(r0, DFT_N2), :]
            taps_ref[0, pl.ds(d0, DFT_N2), :] = _dot_hi(hb, w4f_ref[...]) * win
            taps_ref[1, pl.ds(d0, DFT_N2), :] = _dot_hi(hb, w4b_ref[...]) * jnp.where(t == 0.0, 0.0, win)
            return carry

        lax.fori_loop(0, half, taps_block, 0)
        f1 = f1r_ref[...]

        def stage1(n2, carry):
            x = jnp.concatenate([taps_ref[0, pl.ds(n2, half, stride=DFT_PITCH), :],
                                 taps_ref[1, pl.ds(n2, half, stride=DFT_PITCH), :]], axis=1)
            a = _dot2(f1, x)
            a_ref[0, pl.ds(n2, n1, stride=DFT_PITCH), :] = a[:n1, q0]
            a_ref[1, pl.ds(n2, n1, stride=DFT_PITCH), :] = a[:n1, q1]
            a_ref[2, pl.ds(n2, n1, stride=DFT_PITCH), :] = a[n1:, q0]
            a_ref[3, pl.ds(n2, n1, stride=DFT_PITCH), :] = a[n1:, q1]
            return carry

        lax.fori_loop(0, DFT_N2, stage1, 0, unroll=2)

    f2r = f2r_ref[...]
    f2i = f2i_ref[...]

    def stage2(j, carry):
        k1 = kb * kblk + j
        gr, gi = _twiddled_dft(f2r, f2i, twr_ref[pl.ds(k1, 1), :], twi_ref[pl.ds(k1, 1), :])
        r0 = pl.multiple_of(k1 * DFT_PITCH, 8)
        acat = jnp.concatenate([a_ref[s, pl.ds(r0, DFT_N2), :] for s in range(4)], axis=1)
        r = _dot2(jnp.concatenate([gr, gi], axis=0), acat)
        top, bot = r[:DFT_N2], r[DFT_N2:]
        ff_r, ff_i = top[:, q0] - bot[:, q2], top[:, q2] + bot[:, q0]
        fb_r, fb_i = top[:, q1] - bot[:, q3], top[:, q3] + bot[:, q1]
        o_ref[0, j, :, q0] = ff_r + fb_r
        o_ref[0, j, :, q1] = ff_i - fb_i
        return carry

    lax.fori_loop(0, kblk, stage2, 0, unroll=2)


def _hyena_spec(hmlp, w4, tabs, seq_len):
    n1 = tabs["n1"]
    half = n1 // 2
    fw = hmlp.shape[1]
    cw = w4.shape[1] // 2
    ct = cw // LANES
    kblk = min(DFT_KBLK, n1)
    max_decay = math.log(HYENA_DECAY_TARGET) / HYENA_FAST_PCT
    min_decay = math.log(HYENA_DECAY_TARGET) / HYENA_SLOW_PCT
    deltas = jnp.asarray(np.abs(np.linspace(min_decay, max_decay, cw, dtype=np.float32)).reshape(1, cw))
    full = lambda c, kb: (0, 0)
    sq = pl.BlockSpec((DFT_N2, DFT_N2), full)
    tw = pl.BlockSpec((n1, DFT_N2), full)
    vmem = (2 * seq_len * LANES * 4 + half * DFT_PITCH * 2 * LANES * 4 + n1 * DFT_PITCH * 4 * LANES * 4
            + 2 * kblk * DFT_N2 * 2 * LANES * 4 + 8 * 1024 * 1024)
    w4 = w4.astype(F32)
    return pl.pallas_call(
        functools.partial(_hyena_spec_kernel, seq_len=seq_len, kblk=kblk),
        grid=(ct, n1 // kblk),
        in_specs=[pl.BlockSpec((seq_len, fw), full),
                  pl.BlockSpec((fw, LANES), lambda c, kb: (0, c)),
                  pl.BlockSpec((fw, LANES), lambda c, kb: (0, ct + c)),
                  pl.BlockSpec((1, LANES), lambda c, kb: (0, c)),
                  pl.BlockSpec((2 * n1, half), full), sq, sq, tw, tw],
        out_specs=pl.BlockSpec((1, kblk, DFT_N2, 2 * LANES), lambda c, kb: (c, kb, 0, 0)),
        out_shape=jax.ShapeDtypeStruct((ct, n1, DFT_N2, 2 * LANES), F32),
        scratch_shapes=[pltpu.VMEM((2, half * DFT_PITCH, LANES), F32),
                        pltpu.VMEM((4, n1 * DFT_PITCH, LANES), F32)],
        compiler_params=_cparams(("arbitrary", "arbitrary"), vmem),
        name="hyena_spec",
    )(hmlp, w4, w4, deltas, tabs["f1r"], tabs["f2r"], tabs["f2i"], tabs["twr"], tabs["twi"])


def _conv3_block(ref, bi, r0, nrows, seq_len, w_ref, b_ref):
    lo = pl.multiple_of(jnp.maximum(r0 - BF16_ROWS, 0), BF16_ROWS)
    hi = pl.multiple_of(jnp.minimum(r0 + nrows, seq_len - BF16_ROWS), BF16_ROWS)
    top = ref[bi, pl.ds(lo, BF16_ROWS), :].astype(F32)[BF16_ROWS - 8:] * jnp.where(r0 > 0, 1.0, 0.0)
    mid = ref[bi, pl.ds(pl.multiple_of(r0, BF16_ROWS), nrows), :].astype(F32)
    bot = ref[bi, pl.ds(hi, BF16_ROWS), :].astype(F32)[:8] * jnp.where(r0 + nrows < seq_len, 1.0, 0.0)
    ext = jnp.concatenate([top, mid, bot], axis=0)
    return (ext[7:7 + nrows] * w_ref[0:1, :] + ext[8:8 + nrows] * w_ref[1:2, :]
            + ext[9:9 + nrows] * w_ref[2:3, :] + b_ref[...])


def _hyena_conv_kernel(x0_ref, x1_ref, xv_ref, kf_ref, w0_ref, w1_ref, wv_ref, b0_ref, b1_ref, bv_ref, hb_ref,
                       f1c_ref, f1i_ref, f2r_ref, f2i_ref, twr_ref, twi_ref,
                       o_ref, z_ref, a_ref, *, seq_len, kblk):
    n1 = twr_ref.shape[0]
    half = n1 // 2
    kb = pl.program_id(2)
    nkb = pl.num_programs(2)

    def uh_block(bi, r0):
        return (_conv3_block(x1_ref, bi, r0, DFT_N2, seq_len, w1_ref, b1_ref)
                * _conv3_block(xv_ref, bi, r0, DFT_N2, seq_len, wv_ref, bv_ref))

    @pl.when(kb == 0)
    def _():
        def fill(blk, carry):
            r0 = blk * DFT_N2
            d0 = pl.multiple_of(blk * DFT_PITCH, 8)
            z_ref[0, pl.ds(d0, DFT_N2), :] = uh_block(0, r0)
            z_ref[1, pl.ds(d0, DFT_N2), :] = uh_block(1, r0)
            return carry

        lax.fori_loop(0, half, fill, 0)
        f1 = f1c_ref[...].astype(BF16)

        def stage1(n2, carry):
            ur = z_ref[0, pl.ds(n2, half, stride=DFT_PITCH), :]
            ui = z_ref[1, pl.ds(n2, half, stride=DFT_PITCH), :]
            a = _dot(f1, jnp.concatenate([ur, ui], axis=0).astype(BF16))
            a_ref[0, pl.ds(n2, n1, stride=DFT_PITCH), :] = a[:n1]
            a_ref[1, pl.ds(n2, n1, stride=DFT_PITCH), :] = a[n1:]
            return carry

        lax.fori_loop(0, DFT_N2, stage1, 0, unroll=4)

    f2r = f2r_ref[...]
    f2i = f2i_ref[...]
    lo, hi = slice(0, DFT_N2), slice(DFT_N2, 2 * DFT_N2)

    def stage2(j, carry):
        k1 = kb * kblk + j
        gr, gi = _twiddled_dft(f2r, f2i, twr_ref[pl.ds(k1, 1), :], twi_ref[pl.ds(k1, 1), :])
        r0 = pl.multiple_of(k1 * DFT_PITCH, 8)
        acat = jnp.concatenate([a_ref[0, pl.ds(r0, DFT_N2), :], a_ref[1, pl.ds(r0, DFT_N2), :]], axis=1)
        r = _dot(jnp.concatenate([gr, gi], axis=0).astype(BF16), acat.astype(BF16))
        xr = r[lo, lo] - r[hi, hi]
        xi = r[lo, hi] + r[hi, lo]
        kr = kf_ref[0, j, :, lo]
        ki = kf_ref[0, j, :, hi]
        pcat = jnp.concatenate([xr * kr - xi * ki, xr * ki + xi * kr], axis=1).astype(BF16)
        q = _dot_tn(jnp.concatenate([gr, gi], axis=1).astype(BF16), pcat)
        a_ref[0, pl.ds(r0, DFT_N2), :] = q[lo, lo] + q[hi, hi]
        a_ref[1, pl.ds(r0, DFT_N2), :] = q[lo, hi] - q[hi, lo]
        return carry

    lax.fori_loop(0, kblk, stage2, 0, unroll=4)

    @pl.when(kb == nkb - 1)
    def _():
        f1 = f1i_ref[...].astype(BF16)

        def stage_last(n2, carry):
            br = a_ref[0, pl.ds(n2, n1, stride=DFT_PITCH), :]
            bi = a_ref[1, pl.ds(n2, n1, stride=DFT_PITCH), :]
            y = _dot(f1, jnp.concatenate([br, bi], axis=0).astype(BF16))
            z_ref[0, pl.ds(n2, half, stride=DFT_PITCH), :] = y[:half]
            z_ref[1, pl.ds(n2, half, stride=DFT_PITCH), :] = y[half:]
            return carry

        lax.fori_loop(0, DFT_N2, stage_last, 0, unroll=4)

        def emit(blk, carry):
            r0 = blk * DFT_N2
            d0 = pl.multiple_of(blk * DFT_PITCH, 8)
            for bi in range(2):
                x0 = _conv3_block(x0_ref, bi, r0, DFT_N2, seq_len, w0_ref, b0_ref)
                y = x0 * (z_ref[bi, pl.ds(d0, DFT_N2), :] + hb_ref[...] * uh_block(bi, r0))
                o_ref[bi, pl.ds(pl.multiple_of(r0, DFT_N2), DFT_N2), :] = y.astype(o_ref.dtype)
            return carry

        lax.fori_loop(0, half, emit, 0)


def _hyena_conv(proj, col0, kf, conv_hy, conv_hy_b, hyena_bias, tabs, seq_len):
    b, l, _ = proj.shape
    n1 = tabs["n1"]
    half = n1 // 2
    cw = conv_hy.shape[1] // 3
    ct = cw // LANES
    kblk = min(DFT_KBLK, n1)
    cb0 = col0 // LANES
    cwp = jnp.zeros((8, 3 * cw), F32).at[:HYENA_CONV].set(conv_hy.astype(F32))
    cbp = conv_hy_b.astype(F32).reshape(1, 3 * cw)
    hbp = hyena_bias.astype(F32).reshape(1, cw)
    full = lambda c, p, kb: (0, 0)

    def xspec(g):
        return pl.BlockSpec((2, l, LANES), lambda c, p, kb: (p, 0, cb0 + g * ct + c),
                            pipeline_mode=pl.Buffered(1))

    def wspec(g):
        return pl.BlockSpec((8, LANES), lambda c, p, kb: (0, g * ct + c))

    def bspec(g):
        return pl.BlockSpec((1, LANES), lambda c, p, kb: (0, g * ct + c))

    sq = pl.BlockSpec((DFT_N2, DFT_N2), full)
    tw = pl.BlockSpec((n1, DFT_N2), full)
    vmem = (3 * 2 * l * LANES * 2 + 2 * 2 * l * LANES * 2 + 2 * half * DFT_PITCH * LANES * 4
            + 2 * n1 * DFT_PITCH * LANES * 4 + 2 * kblk * DFT_N2 * 2 * LANES * 4 + 6 * 1024 * 1024)
    return pl.pallas_call(
        functools.partial(_hyena_conv_kernel, seq_len=seq_len, kblk=kblk),
        grid=(ct, b // 2, n1 // kblk),
        in_specs=[xspec(0), xspec(1), xspec(2),
                  pl.BlockSpec((1, kblk, DFT_N2, 2 * LANES), lambda c, p, kb: (c, kb, 0, 0)),
                  wspec(0), wspec(1), wspec(2), bspec(0), bspec(1), bspec(2),
                  pl.BlockSpec((1, LANES), lambda c, p, kb: (0, c)),
                  pl.BlockSpec((2 * n1, n1), full), pl.BlockSpec((n1, 2 * n1), full),
                  sq, sq, tw, tw],
        out_specs=pl.BlockSpec((2, l, LANES), lambda c, p, kb: (p, 0, c)),
        out_shape=jax.ShapeDtypeStruct((b, l, cw), BF16),
        scratch_shapes=[pltpu.VMEM((2, half * DFT_PITCH, LANES), F32),
                        pltpu.VMEM((2, n1 * DFT_PITCH, LANES), F32)],
        compiler_params=_cparams(("arbitrary", "arbitrary", "arbitrary"), vmem),
        name="hyena_conv",
    )(proj, proj, proj, kf, cwp, cwp, cwp, cbp, cbp, cbp, hbp,
      tabs["f1c"], tabs["f1i"], tabs["f2r"], tabs["f2i"], tabs["twr"], tabs["twi"])


def _outproj_kernel(of_ref, ob_ref, z_ref, y_ref, x_ref, ga_ref, gn_ref, nw_ref, w_ref, o_ref):
    parts = []
    for h in range(GDN_HEADS):
        hs = slice(h * GDN_HEAD_DIM, (h + 1) * GDN_HEAD_DIM)
        o = of_ref[0, :, hs].astype(F32) + ob_ref[0, :, hs].astype(F32)
        o = _rms_rows(o, gn_ref[...]) * _silu(z_ref[0, :, hs].astype(F32))
        parts.append(o.astype(BF16))
    parts.append(y_ref[0])
    cat = jnp.concatenate(parts, axis=-1)
    out = _dot(cat, w_ref[...])
    o_ref[0] = x_ref[0] + ga_ref[0] * _rms_rows(out, nw_ref[...])


def _out_proj(o_f, o_b, proj, zcol0, y, x, g_a, gdn_norm, norm_w, w_out, tm):
    b, l, d = x.shape
    zb = zcol0 // GDN_W
    row = lambda bi, i: (bi, i, 0)
    vmem = 4 * 2 * tm * GDN_W * 2 + 4 * tm * d * 4 + 2 * (2 * GDN_W) * d * 2 + 6 * tm * d * 4
    return pl.pallas_call(
        _outproj_kernel,
        grid=(b, l // tm),
        in_specs=[pl.BlockSpec((1, tm, GDN_W), row), pl.BlockSpec((1, tm, GDN_W), row),
                  pl.BlockSpec((1, tm, GDN_W), lambda bi, i: (bi, i, zb)),
                  pl.BlockSpec((1, tm, y.shape[2]), row),
                  pl.BlockSpec((1, tm, d), row),
                  pl.BlockSpec((1, 1, d), lambda bi, i: (bi, 0, 0)),
                  pl.BlockSpec((1, GDN_HEAD_DIM), lambda bi, i: (0, 0)),
                  pl.BlockSpec((1, d), lambda bi, i: (0, 0)),
                  pl.BlockSpec(w_out.shape, lambda bi, i: (0, 0))],
        out_specs=pl.BlockSpec((1, tm, d), row),
        out_shape=jax.ShapeDtypeStruct((b, l, d), F32),
        compiler_params=_cparams(("arbitrary", "arbitrary"), vmem),
        name="out_proj",
    )(o_f, o_b, proj, y, x, g_a, gdn_norm.reshape(1, GDN_HEAD_DIM), norm_w.reshape(1, d), w_out)


def _ffn_kernel(s_ref, sh_ref, sc_ref, gf_ref, npre_ref, npost_ref, wg_ref, wu_ref, wd_ref, o_ref, h_ref, acc_ref):
    j = pl.program_id(2)

    @pl.when(j == 0)
    def _():
        h = _rms_rows(s_ref[0], npre_ref[...]) * (1.0 + sc_ref[0]) + sh_ref[0]
        h_ref[...] = h.astype(BF16)
        acc_ref[...] = jnp.zeros_like(acc_ref)

    hb = h_ref[...]
    act = (_silu(_dot(hb, wg_ref[...])) * _dot(hb, wu_ref[...])).astype(BF16)
    acc_ref[...] += _dot(act, wd_ref[...])

    @pl.when(j == pl.num_programs(2) - 1)
    def _():
        o_ref[0] = s_ref[0] + gf_ref[0] * _rms_rows(acc_ref[...], npost_ref[...])


def _ffn(s, sh, sc, g_f, norm_pre, norm_post, w_gate, w_up, w_down, tm, tf):
    b, l, d = s.shape
    f = w_gate.shape[1]
    row = lambda bi, i, j: (bi, i, 0)
    mod = pl.BlockSpec((1, 1, d), lambda bi, i, j: (bi, 0, 0))
    nrm = pl.BlockSpec((1, d), lambda bi, i, j: (0, 0))
    vmem = 4 * tm * d * 4 + tm * d * 4 + tm * d * 2 + 3 * 2 * d * tf * 2 + 4 * tm * tf * 4 + 2 * tm * d * 4
    return pl.pallas_call(
        _ffn_kernel,
        grid=(b, l // tm, f // tf),
        in_specs=[pl.BlockSpec((1, tm, d), row), mod, mod, mod, nrm, nrm,
                  pl.BlockSpec((d, tf), lambda bi, i, j: (0, j)),
                  pl.BlockSpec((d, tf), lambda bi, i, j: (0, j)),
                  pl.BlockSpec((tf, d), lambda bi, i, j: (j, 0))],
        out_specs=pl.BlockSpec((1, tm, d), row),
        out_shape=jax.ShapeDtypeStruct((b, l, d), F32),
        scratch_shapes=[pltpu.VMEM((tm, d), BF16), pltpu.VMEM((tm, d), F32)],
        compiler_params=_cparams(("arbitrary", "arbitrary", "arbitrary"), vmem),
        name="ffn",
    )(s, sh, sc, g_f, norm_pre.reshape(1, d), norm_post.reshape(1, d), w_gate, w_up, w_down)


COL_Z = 3 * GDN_W
COL_HY = 4 * GDN_W


def _split_w_in(w_in):
    d = w_in.shape[0]
    g0 = 3 * GDN_W
    g1 = g0 + 4 * GDN_HEADS
    w_main = jnp.concatenate([w_in[:, :g0], w_in[:, g1:]], axis=1).astype(BF16)
    w_gate = jnp.zeros((d, LANES), BF16).at[:, :4 * GDN_HEADS].set(w_in[:, g0:g1].astype(BF16))
    return w_main, w_gate


def _row_tile(l, want):
    return want if l % want == 0 else l


def _gdn_branch(x, sh, sc, norm_w, w_main, w_gate, conv_qkv, a_log, dt_bias, s0):
    l = x.shape[1]
    proj, gates = _in_proj(x, sh, sc, norm_w, w_main, w_gate, _row_tile(l, 1024))
    k, v, q, gcol, grow = _gdn_prep(proj, gates, conv_qkv, a_log, dt_bias, _row_tile(l, 256))
    u, w, kd, qg, at, eg = _gdn_chunk(k, v, q, gcol, grow, _row_tile(l, 4 * GDN_CHUNK))
    o_f, o_b, s_fin = _gdn_scan(u, w, kd, qg, at, eg, s0, _row_tile(l, 4 * GDN_CHUNK))
    return proj, o_f, o_b, s_fin


def kernel(x, c, ctx, c_ctx, w_mod, b_mod, norm_pre_mix, norm_post_mix, norm_pre_ffn, norm_post_ffn, w_in, conv_qkv, a_log, dt_bias, gdn_norm, conv_hy, conv_hy_b, filt_w1, filt_b1, filt_freq1, filt_w2, filt_b2, filt_freq2, filt_w3, filt_b3, filt_freq3, filt_w4, hyena_bias, w_out, w_gate, w_up, w_down):
    depth = w_in.shape[0]
    bsz, seq_len, d = x.shape
    assert depth == 1, "the context stream continuation of deeper stacks is not implemented"
    layer = 0
    cvec = jnp.zeros((8, d), F32).at[:bsz].set(c).at[bsz].set(c_ctx)
    mod = _adaln_mod(cvec, w_mod[layer], b_mod[layer])
    sh_a, sc_a, g_a, sh_f, sc_f, g_f = [m.reshape(8, 1, d) for m in jnp.split(mod, 6, axis=-1)]
    ctx_rows = lambda m: jnp.broadcast_to(m[bsz:bsz + 1], (bsz, 1, d))
    w_main, w_gt = _split_w_in(w_in[layer])
    zeros = jnp.zeros((2, bsz, GDN_HEADS, GDN_HEAD_DIM, GDN_HEAD_DIM), F32)
    _, _, _, s_ctx = _gdn_branch(ctx, ctx_rows(sh_a), ctx_rows(sc_a), norm_pre_mix[layer], w_main, w_gt,
                                 conv_qkv[layer], a_log[layer], dt_bias[layer], zeros)
    proj, o_f, o_b, _ = _gdn_branch(x, sh_a[:bsz], sc_a[:bsz], norm_pre_mix[layer], w_main, w_gt,
                                    conv_qkv[layer], a_log[layer], dt_bias[layer], s_ctx)
    tabs = _dft_tables(seq_len)
    hmlp = _hyena_mlp(seq_len, filt_w1[layer], filt_b1[layer], filt_freq1[layer], filt_w2[layer], filt_b2[layer],
                      filt_freq2[layer], filt_w3[layer], filt_b3[layer], filt_freq3[layer])
    kf = _hyena_spec(hmlp, filt_w4[layer], tabs, seq_len)
    y = _hyena_conv(proj, COL_HY, kf, conv_hy[layer], conv_hy_b[layer], hyena_bias[layer], tabs, seq_len)
    s1 = _out_proj(o_f, o_b, proj, COL_Z, y, x, g_a[:bsz], gdn_norm[layer], norm_post_mix[layer],
                   w_out[layer].astype(BF16), _row_tile(seq_len, 512))
    return _ffn(s1, sh_f[:bsz], sc_f[:bsz], g_f[:bsz], norm_pre_ffn[layer], norm_post_ffn[layer],
                w_gate[layer].astype(BF16), w_up[layer].astype(BF16), w_down[layer].astype(BF16),
                _row_tile(seq_len, 512), 512)
```

```python
import functools
import math

import numpy as np
import jax
import jax.numpy as jnp
from jax import lax
from jax.experimental import pallas as pl
from jax.experimental.pallas import tpu as pltpu

F32 = jnp.float32
BF16 = jnp.bfloat16

EPS = 1e-6
GDN_HEADS = 8
GDN_HEAD_DIM = 128
GDN_W = GDN_HEADS * GDN_HEAD_DIM
GDN_CONV = 5
GDN_CHUNK = 64
INV_BLOCK = 16
HYENA_CONV = 3
HYENA_EMB = 33
HYENA_DECAY_TARGET = 1e-2
HYENA_FAST_PCT = 0.3
HYENA_SLOW_PCT = 1.5

LANES = 128
BF16_ROWS = 16
DFT_N2 = 128
DFT_PITCH = 136
DFT_KBLK = 16
V7X_SCOPED_VMEM_CAP = 60000 * 1024


def _cparams(sem, vmem_bytes):
    limit = int(min(max(vmem_bytes, 16 * 1024 * 1024), V7X_SCOPED_VMEM_CAP))
    return pltpu.CompilerParams(dimension_semantics=sem, vmem_limit_bytes=limit)


def _silu(x):
    return x * jax.nn.sigmoid(x)


def _softplus(x):
    return jnp.maximum(x, 0.0) + jnp.log1p(jnp.exp(-jnp.abs(x)))


def _rms_rows(x, w):
    return x * lax.rsqrt(jnp.mean(x * x, axis=-1, keepdims=True) + EPS) * w


def _dot(a, b):
    return jnp.dot(a, b, preferred_element_type=F32)


def _dot_nt(a, b):
    return lax.dot_general(a, b, (((1,), (1,)), ((), ())), preferred_element_type=F32)


def _dot_tn(a, b):
    return lax.dot_general(a, b, (((0,), (0,)), ((), ())), preferred_element_type=F32)


def _dot_hi(a, b):
    return jnp.dot(a, b, preferred_element_type=F32, precision=lax.Precision.HIGHEST)


def _mod_kernel(c_ref, w_ref, b_ref, o_ref):
    a = _silu(c_ref[...]).astype(BF16)
    o_ref[...] = _dot(a, w_ref[...].astype(BF16)) + b_ref[...]


def _adaln_mod(cvec, w_mod, b_mod):
    rows, d = cvec.shape
    n = w_mod.shape[1]
    tn = 1024
    return pl.pallas_call(
        _mod_kernel,
        grid=(n // tn,),
        in_specs=[pl.BlockSpec((rows, d), lambda j: (0, 0)),
                  pl.BlockSpec((d, tn), lambda j: (0, j)),
                  pl.BlockSpec((1, tn), lambda j: (0, j))],
        out_specs=pl.BlockSpec((rows, tn), lambda j: (0, j)),
        out_shape=jax.ShapeDtypeStruct((rows, n), F32),
        compiler_params=_cparams(("arbitrary",), 2 * d * tn * 4 + 4 * d * tn),
        name="adaln_mod",
    )(cvec, w_mod, b_mod.reshape(1, n))


def _inproj_kernel(x_ref, sh_ref, sc_ref, nw_ref, w_ref, wg_ref, o_ref, og_ref, h_ref):
    @pl.when(pl.program_id(2) == 0)
    def _():
        h = _rms_rows(x_ref[0], nw_ref[...]) * (1.0 + sc_ref[0]) + sh_ref[0]
        hb = h.astype(BF16)
        h_ref[...] = hb
        og_ref[0] = _dot(hb, wg_ref[...])

    o_ref[0] = _dot(h_ref[...], w_ref[...]).astype(o_ref.dtype)


def _in_proj(x, sh, sc, norm_w, w_main, w_gate, tm):
    b, l, d = x.shape
    n = w_main.shape[1]
    tn = 1024
    vmem = 2 * tm * d * 4 + 2 * d * tn * 2 + 2 * tm * tn * 2 + tm * d * 2 + 4 * d * LANES * 2 + 4 * tm * d * 4
    return pl.pallas_call(
        _inproj_kernel,
        grid=(b, l // tm, n // tn),
        in_specs=[pl.BlockSpec((1, tm, d), lambda bi, i, j: (bi, i, 0)),
                  pl.BlockSpec((1, 1, d), lambda bi, i, j: (bi, 0, 0)),
                  pl.BlockSpec((1, 1, d), lambda bi, i, j: (bi, 0, 0)),
                  pl.BlockSpec((1, d), lambda bi, i, j: (0, 0)),
                  pl.BlockSpec((d, tn), lambda bi, i, j: (0, j)),
                  pl.BlockSpec((d, LANES), lambda bi, i, j: (0, 0))],
        out_specs=[pl.BlockSpec((1, tm, tn), lambda bi, i, j: (bi, i, j)),
                   pl.BlockSpec((1, tm, LANES), lambda bi, i, j: (bi, i, 0))],
        out_shape=[jax.ShapeDtypeStruct((b, l, n), BF16),
                   jax.ShapeDtypeStruct((b, l, LANES), F32)],
        scratch_shapes=[pltpu.VMEM((tm, d), BF16)],
        compiler_params=_cparams(("arbitrary", "arbitrary", "arbitrary"), vmem),
        name="in_proj",
    )(x, sh, sc, norm_w.reshape(1, d), w_main, w_gate)


def _gdn_prep_kernel(p_ref, pp_ref, pn_ref, ba_ref, cw_ref, gp_ref,
                     k_ref, v_ref, q_ref, gc_ref, gr_ref):
    i = pl.program_id(1)
    ts = p_ref.shape[1]
    has_prev = (i > 0).astype(F32)
    has_next = (i < pl.num_programs(1) - 1).astype(F32)
    pad = GDN_CONV // 2
    outs = (k_ref, v_ref, q_ref)
    for t in range(3 * GDN_HEADS):
        cs = t * LANES
        top = pp_ref[0, :, cs:cs + LANES].astype(F32)[BF16_ROWS - 8:] * has_prev
        mid = p_ref[0, :, cs:cs + LANES].astype(F32)
        bot = pn_ref[0, :, cs:cs + LANES].astype(F32)[:8] * has_next
        ext = jnp.concatenate([top, mid, bot], axis=0)
        acc = ext[8 - pad:8 - pad + ts] * cw_ref[0:1, cs:cs + LANES]
        for d in range(1, GDN_CONV):
            acc = acc + ext[8 - pad + d:8 - pad + d + ts] * cw_ref[d:d + 1, cs:cs + LANES]
        y = _silu(acc)
        grp = t // GDN_HEADS
        hs = (t % GDN_HEADS) * LANES
        if grp == 0:
            y = y * lax.rsqrt(jnp.sum(y * y, axis=-1, keepdims=True) + EPS)
        elif grp == 2:
            y = y * (lax.rsqrt(jnp.sum(y * y, axis=-1, keepdims=True) + EPS) * (GDN_HEAD_DIM ** -0.5))
        outs[grp][0, :, hs:hs + LANES] = y.astype(outs[grp].dtype)

    ba = ba_ref[0]
    lane = lax.broadcasted_iota(jnp.int32, ba.shape, 1)
    beta = jax.nn.sigmoid(ba)
    g = -gp_ref[0:1, :] * _softplus(ba + gp_ref[1:2, :])
    g = jnp.where((lane >= 2 * GDN_HEADS) & (lane < 4 * GDN_HEADS), g, 0.0)
    r = lax.broadcasted_iota(jnp.int32, (GDN_CHUNK, GDN_CHUNK), 0)
    c = lax.broadcasted_iota(jnp.int32, (GDN_CHUNK, GDN_CHUNK), 1)
    lower = (r >= c).astype(F32)
    upper = (r <= c).astype(F32)
    lane_c = lax.broadcasted_iota(jnp.int32, (GDN_CHUNK, LANES), 1)
    for ci in range(ts // GDN_CHUNK):
        rows = slice(ci * GDN_CHUNK, (ci + 1) * GDN_CHUNK)
        gch = g[rows]
        cf = _dot_hi(lower, gch)
        cb = _dot_hi(upper, gch)
        packed = jnp.where(lane_c < 2 * GDN_HEADS, beta[rows],
                           jnp.where(lane_c < 3 * GDN_HEADS, cf, cb))
        gc_ref[0, rows, :] = packed
        gr_ref[0, ci] = packed.T[0:4 * GDN_HEADS, :]


def _gdn_prep(proj, gates, conv_qkv, a_log, dt_bias, ts):
    b, l, _ = proj.shape
    w3 = 3 * GDN_W
    hb = BF16_ROWS
    nblk = l // ts
    cw = jnp.zeros((8, w3), F32).at[:GDN_CONV].set(conv_qkv.astype(F32))
    gp = jnp.zeros((8, LANES), F32)
    gp = gp.at[0, 2 * GDN_HEADS:4 * GDN_HEADS].set(jnp.exp(a_log.astype(F32)).reshape(-1))
    gp = gp.at[1, 2 * GDN_HEADS:4 * GDN_HEADS].set(dt_bias.astype(F32).reshape(-1))
    last_hb = l // hb - 1
    out_kvq = jax.ShapeDtypeStruct((b, l, GDN_W), BF16)
    return pl.pallas_call(
        _gdn_prep_kernel,
        grid=(b, nblk),
        in_specs=[pl.BlockSpec((1, ts, w3), lambda bi, i: (bi, i, 0)),
                  pl.BlockSpec((1, hb, w3), lambda bi, i: (bi, jnp.maximum(i * (ts // hb) - 1, 0), 0)),
                  pl.BlockSpec((1, hb, w3), lambda bi, i: (bi, jnp.minimum((i + 1) * (ts // hb), last_hb), 0)),
                  pl.BlockSpec((1, ts, LANES), lambda bi, i: (bi, i, 0)),
                  pl.BlockSpec((8, w3), lambda bi, i: (0, 0)),
                  pl.BlockSpec((8, LANES), lambda bi, i: (0, 0))],
        out_specs=[pl.BlockSpec((1, ts, GDN_W), lambda bi, i: (bi, i, 0)),
                   pl.BlockSpec((1, ts, GDN_W), lambda bi, i: (bi, i, 0)),
                   pl.BlockSpec((1, ts, GDN_W), lambda bi, i: (bi, i, 0)),
                   pl.BlockSpec((1, ts, LANES), lambda bi, i: (bi, i, 0)),
                   pl.BlockSpec((1, ts // GDN_CHUNK, 4 * GDN_HEADS, GDN_CHUNK), lambda bi, i: (bi, i, 0, 0))],
        out_shape=[out_kvq, out_kvq, out_kvq,
                   jax.ShapeDtypeStruct((b, l, LANES), F32),
                   jax.ShapeDtypeStruct((b, l // GDN_CHUNK, 4 * GDN_HEADS, GDN_CHUNK), F32)],
        compiler_params=_cparams(("arbitrary", "arbitrary"), 32 * 1024 * 1024),
        name="gdn_prep",
    )(proj, proj, proj, gates, cw, gp)


def _bmm(a, b):
    return jnp.einsum('gmk,gkn->gmn', a, b, preferred_element_type=F32)


def _bmm_nt(a, b):
    return jnp.einsum('gmk,gnk->gmn', a, b, preferred_element_type=F32)


def _bmm_tn(a, b):
    return jnp.einsum('gkm,gkn->gmn', a, b, preferred_element_type=F32)


def _tri_inverse(a, blockdiag):
    n = a.shape[-1]
    r = lax.broadcasted_iota(jnp.int32, (n, n), 0)
    c = lax.broadcasted_iota(jnp.int32, (n, n), 1)
    eye = (r == c).astype(F32)
    d = jnp.where(blockdiag, a, 0.0)
    e = a - d
    db = d.astype(BF16)
    d2b = _bmm(db, db).astype(BF16)
    d4b = _bmm(d2b, d2b).astype(BF16)
    d8b = _bmm(d4b, d4b).astype(BF16)
    x = eye - d
    x = x + _bmm(x.astype(BF16), d2b)
    x = x + _bmm(x.astype(BF16), d4b)
    tdiag = x + _bmm(x.astype(BF16), d8b)
    tdb = tdiag.astype(BF16)
    nb = _bmm(tdb, e.astype(BF16)).astype(BF16)
    n2b = _bmm(nb, nb).astype(BF16)
    y = tdiag + _bmm(n2b, tdb)
    return y - _bmm(nb, y.astype(BF16))


def _gdn_chunk_kernel(k_ref, v_ref, q_ref, gc_ref, gr_ref, u_ref, w_ref, kd_ref, qg_ref, at_ref, eg_ref):
    n = GDN_CHUNK
    nch = k_ref.shape[1] // n
    r = lax.broadcasted_iota(jnp.int32, (n, n), 0)
    c = lax.broadcasted_iota(jnp.int32, (n, n), 1)
    blockdiag = (r // INV_BLOCK) == (c // INV_BLOCK)
    strict = (r > c, r < c)
    incl = (r >= c, r <= c)
    pairs = [(ci, h) for ci in range(nch) for h in range(GDN_HEADS)]
    np_ = len(pairs)

    def tile(ref, ci, h):
        return ref[0, ci * n:(ci + 1) * n, h * GDN_HEAD_DIM:(h + 1) * GDN_HEAD_DIM]

    ks = jnp.stack([tile(k_ref, ci, h) for ci, h in pairs])
    vs = jnp.stack([tile(v_ref, ci, h) for ci, h in pairs])
    qs = jnp.stack([tile(q_ref, ci, h) for ci, h in pairs])
    kk = _bmm_nt(ks, ks)
    qk = _bmm_nt(qs, ks)
    gcol = gc_ref[0]
    a_parts, at_parts, gcb, brow, grow = [], [], [], [], []
    for d in range(2):
        off = d * GDN_HEADS
        cols = lambda lane: jnp.stack([jnp.broadcast_to(gcol[ci * n:(ci + 1) * n, lane + h:lane + h + 1], (n, LANES))
                                       for ci, h in pairs])
        rows = lambda row: jnp.stack([gr_ref[0, ci, row + h:row + h + 1, :] for ci, h in pairs])
        gcb_d = cols(2 * GDN_HEADS + off)
        beta_c = cols(off)[:, :, :n]
        gc_r = rows(2 * GDN_HEADS + off)
        dec = jnp.exp(jnp.where(incl[d], gcb_d[:, :, :n] - gc_r, 0.0))
        a_parts.append(jnp.where(strict[d], beta_c * kk * dec, 0.0))
        at_parts.append(jnp.where(incl[d], qk * dec, 0.0).astype(BF16))
        gcb.append(gcb_d)
        brow.append(rows(off))
        grow.append(gc_r)
    t = _tri_inverse(jnp.concatenate(a_parts, axis=0), blockdiag)
    tb = t * jnp.concatenate(brow, axis=0)
    k2 = jnp.concatenate([ks, ks], axis=0)
    u = _bmm(tb.astype(BF16), jnp.concatenate([vs, vs], axis=0))
    w = _bmm((tb * jnp.exp(jnp.concatenate(grow, axis=0))).astype(BF16), k2)
    kf = ks.astype(F32)
    qf = qs.astype(F32)
    for d in range(2):
        last = 0 if d else n - 1
        g_last = gcb[d][:, last:last + 1, :]
        kd = (kf * jnp.exp(g_last - gcb[d])).astype(BF16)
        qg = (qf * jnp.exp(gcb[d])).astype(BF16)
        eg = jnp.exp(g_last)
        for p, (ci, h) in enumerate(pairs):
            rs = slice(ci * n, (ci + 1) * n)
            hs = slice(h * GDN_HEAD_DIM, (h + 1) * GDN_HEAD_DIM)
            u_ref[d, 0, rs, hs] = u[d * np_ + p]
            w_ref[d, 0, rs, hs] = w[d * np_ + p].astype(BF16)
            kd_ref[d, 0, rs, hs] = kd[p]
            qg_ref[d, 0, rs, hs] = qg[p]
            at_ref[d, 0, h, rs, :] = at_parts[d][p]
            eg_ref[d, 0, ci, h:h + 1, :] = eg[p]


def _gdn_chunk(k, v, q, gcol, grow, ts):
    b, l, _ = k.shape
    nc = l // GDN_CHUNK
    row = lambda bi, i: (bi, i, 0)
    orow = lambda bi, i: (0, bi, i, 0)
    wide = pl.BlockSpec((2, 1, ts, GDN_W), orow)
    shp = lambda dt: jax.ShapeDtypeStruct((2, b, l, GDN_W), dt)
    return pl.pallas_call(
        _gdn_chunk_kernel,
        grid=(b, l // ts),
        in_specs=[pl.BlockSpec((1, ts, GDN_W), row), pl.BlockSpec((1, ts, GDN_W), row),
                  pl.BlockSpec((1, ts, GDN_W), row), pl.BlockSpec((1, ts, LANES), row),
                  pl.BlockSpec((1, ts // GDN_CHUNK, 4 * GDN_HEADS, GDN_CHUNK), lambda bi, i: (bi, i, 0, 0))],
        out_specs=[wide, wide, wide, wide,
                   pl.BlockSpec((2, 1, GDN_HEADS, ts, GDN_CHUNK), lambda bi, i: (0, bi, 0, i, 0)),
                   pl.BlockSpec((2, 1, ts // GDN_CHUNK, GDN_HEADS, LANES), lambda bi, i: (0, bi, i, 0, 0))],
        out_shape=[shp(F32), shp(BF16), shp(BF16), shp(BF16),
                   jax.ShapeDtypeStruct((2, b, GDN_HEADS, l, GDN_CHUNK), BF16),
                   jax.ShapeDtypeStruct((2, b, nc, GDN_HEADS, LANES), F32)],
        compiler_params=_cparams(("arbitrary", "arbitrary"), 32 * 1024 * 1024),
        name="gdn_chunk",
    )(k, v, q, gcol, grow)


def _gdn_scan_kernel(uf, wf, kdf, qgf, atf, egf, ub, wb, kdb, qgb, atb, egb, s0_ref,
                     of_ref, ob_ref, sfin_ref, s_ref):
    i = pl.program_id(1)
    n = GDN_CHUNK
    nch = uf.shape[2] // n
    ng = 2 * GDN_HEADS

    @pl.when(i == 0)
    def _():
        s_ref[...] = s0_ref[:, 0].reshape(ng, GDN_HEAD_DIM, GDN_HEAD_DIM)

    refs = ((uf, wf, kdf, qgf, atf, egf, of_ref), (ub, wb, kdb, qgb, atb, egb, ob_ref))
    for step in range(nch):
        cis = (step, nch - 1 - step)

        def gather(idx, head_major=False):
            out = []
            for d in range(2):
                rs = slice(cis[d] * n, (cis[d] + 1) * n)
                for h in range(GDN_HEADS):
                    if head_major:
                        out.append(refs[d][idx][0, 0, h, rs, :])
                    else:
                        out.append(refs[d][idx][0, 0, rs, h * GDN_HEAD_DIM:(h + 1) * GDN_HEAD_DIM])
            return jnp.stack(out)

        u, w, kd, qg = gather(0), gather(1), gather(2), gather(3)
        at = gather(4, head_major=True)
        eg = jnp.stack([refs[d][5][0, 0, cis[d], h:h + 1, :] for d in range(2) for h in range(GDN_HEADS)])
        s = s_ref[...]
        sb = s.astype(BF16)
        v_new = (u - _bmm(w, sb)).astype(BF16)
        o = _bmm(qg, sb) + _bmm(at, v_new)
        s_ref[...] = s * eg + _bmm_tn(kd, v_new)
        for d in range(2):
            rs = slice(cis[d] * n, (cis[d] + 1) * n)
            for h in range(GDN_HEADS):
                refs[d][6][0, rs, h * GDN_HEAD_DIM:(h + 1) * GDN_HEAD_DIM] = o[d * GDN_HEADS + h].astype(BF16)

    @pl.when(i == pl.num_programs(1) - 1)
    def _():
        sfin_ref[:, 0] = s_ref[...].reshape(2, GDN_HEADS, GDN_HEAD_DIM, GDN_HEAD_DIM)


def _gdn_scan(u, w, kd, qg, at, eg, s0, ts):
    _, b, l, _ = u.shape
    nb = l // ts
    nch = ts // GDN_CHUNK

    def specs(d):
        blk = (lambda i: i) if d == 0 else (lambda i: nb - 1 - i)
        wide = pl.BlockSpec((1, 1, ts, GDN_W), lambda bi, i: (d, bi, blk(i), 0))
        return [wide, wide, wide, wide,
                pl.BlockSpec((1, 1, GDN_HEADS, ts, GDN_CHUNK), lambda bi, i: (d, bi, 0, blk(i), 0)),
                pl.BlockSpec((1, 1, nch, GDN_HEADS, LANES), lambda bi, i: (d, bi, blk(i), 0, 0))]

    sblk = (2, 1, GDN_HEADS, GDN_HEAD_DIM, GDN_HEAD_DIM)
    smap = lambda bi, i: (0, bi, 0, 0, 0)
    oshape = jax.ShapeDtypeStruct((b, l, GDN_W), BF16)
    args = (u, w, kd, qg, at, eg)
    return pl.pallas_call(
        _gdn_scan_kernel,
        grid=(b, nb),
        in_specs=specs(0) + specs(1) + [pl.BlockSpec(sblk, smap)],
        out_specs=[pl.BlockSpec((1, ts, GDN_W), lambda bi, i: (bi, i, 0)),
                   pl.BlockSpec((1, ts, GDN_W), lambda bi, i: (bi, nb - 1 - i, 0)),
                   pl.BlockSpec(sblk, smap)],
        out_shape=[oshape, oshape,
                   jax.ShapeDtypeStruct((2, b, GDN_HEADS, GDN_HEAD_DIM, GDN_HEAD_DIM), F32)],
        scratch_shapes=[pltpu.VMEM((2 * GDN_HEADS, GDN_HEAD_DIM, GDN_HEAD_DIM), F32)],
        compiler_params=_cparams(("arbitrary", "arbitrary"), 32 * 1024 * 1024),
        name="gdn_scan",
    )(*args, *args, s0)


def _hyena_mlp_kernel(fr_ref, w1_ref, b1_ref, f1_ref, w2_ref, b2_ref, f2_ref, w3_ref, b3_ref, f3_ref,
                      o_ref, *, seq_len):
    tl = o_ref.shape[0]
    i = pl.program_id(0)
    row = (lax.broadcasted_iota(jnp.int32, (tl, LANES), 0) + i * tl).astype(F32)
    lane = lax.broadcasted_iota(jnp.int32, (tl, LANES), 1)
    bands = (HYENA_EMB - 1) // 2
    t01 = row / max(seq_len - 1, 1)
    ang = (2.0 * math.pi / seq_len) * row * fr_ref[...]
    z = jnp.where(lane == 0, t01,
                  jnp.where(lane <= bands, jnp.cos(ang),
                            jnp.where(lane <= 2 * bands, -jnp.sin(ang), 0.0)))
    h = jnp.sin(f1_ref[...] * (_dot_hi(z, w1_ref[...]) + b1_ref[...]))
    h = jnp.sin(f2_ref[...] * (_dot_hi(h, w2_ref[...]) + b2_ref[...]))
    h = jnp.sin(f3_ref[...] * (_dot_hi(h, w3_ref[...]) + b3_ref[...]))
    o_ref[...] = h


def _hyena_mlp(seq_len, w1, b1, f1, w2, b2, f2, w3, b3, f3):
    fw = w2.shape[0]
    bands = (HYENA_EMB - 1) // 2
    freqs = np.linspace(1e-4, bands - 1, bands, dtype=np.float32)
    fr = np.zeros((1, LANES), np.float32)
    fr[0, 1:1 + bands] = freqs
    fr[0, 1 + bands:1 + 2 * bands] = freqs
    w1p = jnp.zeros((LANES, fw), F32).at[:HYENA_EMB].set(w1.astype(F32))
    tl = min(seq_len, 1024)
    full = lambda i: (0, 0)
    vec = pl.BlockSpec((1, fw), full)
    mat = pl.BlockSpec((fw, fw), full)
    r2 = lambda a: a.astype(F32).reshape(1, fw)
    return pl.pallas_call(
        functools.partial(_hyena_mlp_kernel, seq_len=seq_len),
        grid=(seq_len // tl,),
        in_specs=[pl.BlockSpec((1, LANES), full), pl.BlockSpec((LANES, fw), full), vec, vec,
                  mat, vec, vec, mat, vec, vec],
        out_specs=pl.BlockSpec((tl, fw), lambda i: (i, 0)),
        out_shape=jax.ShapeDtypeStruct((seq_len, fw), F32),
        compiler_params=_cparams(("arbitrary",), 16 * 1024 * 1024),
        name="hyena_mlp",
    )(jnp.asarray(fr), w1p, r2(b1), r2(f1), w2.astype(F32), r2(b2), r2(f2), w3.astype(F32), r2(b3), r2(f3))


def _dft_tables(seq_len):
    n1 = 2 * seq_len // DFT_N2
    n = n1 * DFT_N2
    half = n1 // 2
    k1 = np.arange(n1, dtype=np.float64)[:, None]
    a1 = 2.0 * np.pi * k1 * np.arange(n1, dtype=np.float64)[None, :] / n1
    c1, s1 = np.cos(a1), np.sin(a1)
    f1c = np.block([[c1[:, :half], s1[:, :half]], [-s1[:, :half], c1[:, :half]]])
    f1r = np.concatenate([c1[:, :half], -s1[:, :half]], axis=0)
    ct, st = c1.T[:half], s1.T[:half]
    f1i = np.block([[ct, -st], [st, ct]]) / n
    a2 = 2.0 * np.pi * np.arange(DFT_N2, dtype=np.float64)[:, None] * np.arange(DFT_N2, dtype=np.float64)[None, :] / DFT_N2
    atw = 2.0 * np.pi * k1 * np.arange(DFT_N2, dtype=np.float64)[None, :] / n
    as32 = lambda a: jnp.asarray(a.astype(np.float32))
    return dict(n1=n1, f1c=as32(f1c), f1r=as32(f1r), f1i=as32(f1i),
                f2r=as32(np.cos(a2)), f2i=as32(-np.sin(a2)),
                twr=as32(np.cos(atw)), twi=as32(-np.sin(atw)))


def _twiddled_dft(f2r, f2i, twr_row, twi_row):
    return f2r * twr_row - f2i * twi_row, f2r * twi_row + f2i * twr_row


def _split_bf16(x):
    hi = x.astype(BF16)
    return hi, (x - hi.astype(F32)).astype(BF16)


def _dot2(a, b):
    ah = a.astype(BF16)
    bh, bl = _split_bf16(b)
    return _dot(ah, bh) + _dot(ah, bl)


def _hyena_spec_kernel(h_ref, w4f_ref, w4b_ref, dl_ref, bias_ref, f1r_ref, f2r_ref, f2i_ref, twr_ref, twi_ref,
                       o_ref, taps_ref, a_ref, *, seq_len, kblk):
    n1 = twr_ref.shape[0]
    half = n1 // 2
    kb = pl.program_id(1)
    q0, q1, q2, q3 = (slice(i * LANES, (i + 1) * LANES) for i in range(4))

    @pl.when(kb == 0)
    def _():
        def taps_block(blk, carry):
            r0 = pl.multiple_of(blk * DFT_N2, DFT_N2)
            t = (lax.broadcasted_iota(jnp.int32, (DFT_N2, LANES), 0) + r0).astype(F32)
            win = jnp.exp(-(t / max(seq_len - 1, 1)) * dl_ref[...])
            d0 = pl.multiple_of(blk * DFT_PITCH, 8)
            hb = h_ref[pl.ds(r0, DFT_N2), :]
            taps_ref[0, pl.ds(d0, DFT_N2), :] = (_dot_hi(hb, w4f_ref[...]) * win
                                                 + jnp.where(t == 0.0, bias_ref[...], 0.0))
            taps_ref[1, pl.ds(d0, DFT_N2), :] = _dot_hi(hb, w4b_ref[...]) * jnp.where(t == 0.0, 0.0, win)
            return carry

        lax.fori_loop(0, half, taps_block, 0)
        f1 = f1r_ref[...]

        def stage1(n2, carry):
            x = jnp.concatenate([taps_ref[0, pl.ds(n2, half, stride=DFT_PITCH), :],
                                 taps_ref[1, pl.ds(n2, half, stride=DFT_PITCH), :]], axis=1)
            a = _dot2(f1, x)
            a_ref[0, pl.ds(n2, n1, stride=DFT_PITCH), :] = a[:n1, q0]
            a_ref[1, pl.ds(n2, n1, stride=DFT_PITCH), :] = a[:n1, q1]
            a_ref[2, pl.ds(n2, n1, stride=DFT_PITCH), :] = a[n1:, q0]
            a_ref[3, pl.ds(n2, n1, stride=DFT_PITCH), :] = a[n1:, q1]
            return carry

        lax.fori_loop(0, DFT_N2, stage1, 0, unroll=2)

    f2r = f2r_ref[...]
    f2i = f2i_ref[...]

    def stage2(j, carry):
        k1 = kb * kblk + j
        gr, gi = _twiddled_dft(f2r, f2i, twr_ref[pl.ds(k1, 1), :], twi_ref[pl.ds(k1, 1), :])
        r0 = pl.multiple_of(k1 * DFT_PITCH, 8)
        acat = jnp.concatenate([a_ref[s, pl.ds(r0, DFT_N2), :] for s in range(4)], axis=1)
        r = _dot(jnp.concatenate([gr, gi], axis=0).astype(BF16), acat.astype(BF16))
        top, bot = r[:DFT_N2], r[DFT_N2:]
        ff_r, ff_i = top[:, q0] - bot[:, q2], top[:, q2] + bot[:, q0]
        fb_r, fb_i = top[:, q1] - bot[:, q3], top[:, q3] + bot[:, q1]
        o_ref[0, j, :, q0] = ff_r + fb_r
        o_ref[0, j, :, q1] = ff_i - fb_i
        return carry

    lax.fori_loop(0, kblk, stage2, 0, unroll=2)


def _hyena_spec(hmlp, w4, hyena_bias, tabs, seq_len):
    n1 = tabs["n1"]
    half = n1 // 2
    fw = hmlp.shape[1]
    cw = w4.shape[1] // 2
    ct = cw // LANES
    kblk = min(DFT_KBLK, n1)
    max_decay = math.log(HYENA_DECAY_TARGET) / HYENA_FAST_PCT
    min_decay = math.log(HYENA_DECAY_TARGET) / HYENA_SLOW_PCT
    deltas = jnp.asarray(np.abs(np.linspace(min_decay, max_decay, cw, dtype=np.float32)).reshape(1, cw))
    full = lambda c, kb: (0, 0)
    sq = pl.BlockSpec((DFT_N2, DFT_N2), full)
    tw = pl.BlockSpec((n1, DFT_N2), full)
    vmem = (2 * seq_len * LANES * 4 + half * DFT_PITCH * 2 * LANES * 4 + n1 * DFT_PITCH * 4 * LANES * 4
            + 2 * kblk * DFT_N2 * 2 * LANES * 4 + 8 * 1024 * 1024)
    w4 = w4.astype(F32)
    return pl.pallas_call(
        functools.partial(_hyena_spec_kernel, seq_len=seq_len, kblk=kblk),
        grid=(ct, n1 // kblk),
        in_specs=[pl.BlockSpec((seq_len, fw), full),
                  pl.BlockSpec((fw, LANES), lambda c, kb: (0, c)),
                  pl.BlockSpec((fw, LANES), lambda c, kb: (0, ct + c)),
                  pl.BlockSpec((1, LANES), lambda c, kb: (0, c)),
                  pl.BlockSpec((1, LANES), lambda c, kb: (0, c)),
                  pl.BlockSpec((2 * n1, half), full), sq, sq, tw, tw],
        out_specs=pl.BlockSpec((1, kblk, DFT_N2, 2 * LANES), lambda c, kb: (c, kb, 0, 0)),
        out_shape=jax.ShapeDtypeStruct((ct, n1, DFT_N2, 2 * LANES), F32),
        scratch_shapes=[pltpu.VMEM((2, half * DFT_PITCH, LANES), F32),
                        pltpu.VMEM((4, n1 * DFT_PITCH, LANES), F32)],
        compiler_params=_cparams(("arbitrary", "arbitrary"), vmem),
        name="hyena_spec",
    )(hmlp, w4, w4, deltas, hyena_bias.astype(F32).reshape(1, cw),
      tabs["f1r"], tabs["f2r"], tabs["f2i"], tabs["twr"], tabs["twi"])


def _conv3_block(ref, bi, r0, nrows, seq_len, w_ref, b_ref):
    lo = pl.multiple_of(jnp.maximum(r0 - BF16_ROWS, 0), BF16_ROWS)
    hi = pl.multiple_of(jnp.minimum(r0 + nrows, seq_len - BF16_ROWS), BF16_ROWS)
    top = ref[bi, pl.ds(lo, BF16_ROWS), :].astype(F32)[BF16_ROWS - 8:] * jnp.where(r0 > 0, 1.0, 0.0)
    mid = ref[bi, pl.ds(pl.multiple_of(r0, BF16_ROWS), nrows), :].astype(F32)
    bot = ref[bi, pl.ds(hi, BF16_ROWS), :].astype(F32)[:8] * jnp.where(r0 + nrows < seq_len, 1.0, 0.0)
    ext = jnp.concatenate([top, mid, bot], axis=0)
    return (ext[7:7 + nrows] * w_ref[0:1, :] + ext[8:8 + nrows] * w_ref[1:2, :]
            + ext[9:9 + nrows] * w_ref[2:3, :] + b_ref[...])


def _hyena_conv_kernel(x0_ref, x1_ref, xv_ref, kf_ref, w0_ref, w1_ref, wv_ref, b0_ref, b1_ref, bv_ref,
                       f1c_ref, f1i_ref, f2r_ref, f2i_ref, twr_ref, twi_ref,
                       o_ref, z_ref, a_ref, *, seq_len, kblk):
    n1 = twr_ref.shape[0]
    half = n1 // 2
    kb = pl.program_id(2)
    nkb = pl.num_programs(2)

    def uh_block(bi, r0):
        return (_conv3_block(x1_ref, bi, r0, DFT_N2, seq_len, w1_ref, b1_ref)
                * _conv3_block(xv_ref, bi, r0, DFT_N2, seq_len, wv_ref, bv_ref))

    @pl.when(kb == 0)
    def _():
        def fill(blk, carry):
            r0 = blk * DFT_N2
            d0 = pl.multiple_of(blk * DFT_PITCH, 8)
            z_ref[0, pl.ds(d0, DFT_N2), :] = uh_block(0, r0)
            z_ref[1, pl.ds(d0, DFT_N2), :] = uh_block(1, r0)
            return carry

        lax.fori_loop(0, half, fill, 0)
        f1 = f1c_ref[...].astype(BF16)

        def stage1(n2, carry):
            ur = z_ref[0, pl.ds(n2, half, stride=DFT_PITCH), :]
            ui = z_ref[1, pl.ds(n2, half, stride=DFT_PITCH), :]
            a = _dot(f1, jnp.concatenate([ur, ui], axis=0).astype(BF16))
            a_ref[0, pl.ds(n2, n1, stride=DFT_PITCH), :] = a[:n1]
            a_ref[1, pl.ds(n2, n1, stride=DFT_PITCH), :] = a[n1:]
            return carry

        lax.fori_loop(0, DFT_N2, stage1, 0, unroll=4)

    f2r = f2r_ref[...]
    f2i = f2i_ref[...]
    lo, hi = slice(0, DFT_N2), slice(DFT_N2, 2 * DFT_N2)

    def stage2(j, carry):
        k1 = kb * kblk + j
        gr, gi = _twiddled_dft(f2r, f2i, twr_ref[pl.ds(k1, 1), :], twi_ref[pl.ds(k1, 1), :])
        r0 = pl.multiple_of(k1 * DFT_PITCH, 8)
        acat = jnp.concatenate([a_ref[0, pl.ds(r0, DFT_N2), :], a_ref[1, pl.ds(r0, DFT_N2), :]], axis=1)
        r = _dot(jnp.concatenate([gr, gi], axis=0).astype(BF16), acat.astype(BF16))
        xr = r[lo, lo] - r[hi, hi]
        xi = r[lo, hi] + r[hi, lo]
        kr = kf_ref[0, j, :, lo]
        ki = kf_ref[0, j, :, hi]
        pcat = jnp.concatenate([xr * kr - xi * ki, xr * ki + xi * kr], axis=1).astype(BF16)
        q = _dot_tn(jnp.concatenate([gr, gi], axis=1).astype(BF16), pcat)
        a_ref[0, pl.ds(r0, DFT_N2), :] = q[lo, lo] + q[hi, hi]
        a_ref[1, pl.ds(r0, DFT_N2), :] = q[lo, hi] - q[hi, lo]
        return carry

    lax.fori_loop(0, kblk, stage2, 0, unroll=4)

    @pl.when(kb == nkb - 1)
    def _():
        f1 = f1i_ref[...].astype(BF16)

        def stage_last(n2, carry):
            br = a_ref[0, pl.ds(n2, n1, stride=DFT_PITCH), :]
            bi = a_ref[1, pl.ds(n2, n1, stride=DFT_PITCH), :]
            y = _dot(f1, jnp.concatenate([br, bi], axis=0).astype(BF16))
            z_ref[0, pl.ds(n2, half, stride=DFT_PITCH), :] = y[:half]
            z_ref[1, pl.ds(n2, half, stride=DFT_PITCH), :] = y[half:]
            return carry

        lax.fori_loop(0, DFT_N2, stage_last, 0, unroll=4)

        def emit(blk, carry):
            r0 = blk * DFT_N2
            d0 = pl.multiple_of(blk * DFT_PITCH, 8)
            for bi in range(2):
                x0 = _conv3_block(x0_ref, bi, r0, DFT_N2, seq_len, w0_ref, b0_ref)
                y = x0 * z_ref[bi, pl.ds(d0, DFT_N2), :]
                o_ref[bi, pl.ds(pl.multiple_of(r0, DFT_N2), DFT_N2), :] = y.astype(o_ref.dtype)
            return carry

        lax.fori_loop(0, half, emit, 0)


def _hyena_conv(proj, col0, kf, conv_hy, conv_hy_b, tabs, seq_len):
    b, l, _ = proj.shape
    n1 = tabs["n1"]
    half = n1 // 2
    cw = conv_hy.shape[1] // 3
    ct = cw // LANES
    kblk = min(DFT_KBLK, n1)
    cb0 = col0 // LANES
    cwp = jnp.zeros((8, 3 * cw), F32).at[:HYENA_CONV].set(conv_hy.astype(F32))
    cbp = conv_hy_b.astype(F32).reshape(1, 3 * cw)
    full = lambda c, p, kb: (0, 0)

    def xspec(g):
        return pl.BlockSpec((2, l, LANES), lambda c, p, kb: (p, 0, cb0 + g * ct + c),
                            pipeline_mode=pl.Buffered(1))

    def wspec(g):
        return pl.BlockSpec((8, LANES), lambda c, p, kb: (0, g * ct + c))

    def bspec(g):
        return pl.BlockSpec((1, LANES), lambda c, p, kb: (0, g * ct + c))

    sq = pl.BlockSpec((DFT_N2, DFT_N2), full)
    tw = pl.BlockSpec((n1, DFT_N2), full)
    vmem = (3 * 2 * l * LANES * 2 + 2 * 2 * l * LANES * 2 + 2 * half * DFT_PITCH * LANES * 4
            + 2 * n1 * DFT_PITCH * LANES * 4 + 2 * kblk * DFT_N2 * 2 * LANES * 4 + 6 * 1024 * 1024)
    return pl.pallas_call(
        functools.partial(_hyena_conv_kernel, seq_len=seq_len, kblk=kblk),
        grid=(ct, b // 2, n1 // kblk),
        in_specs=[xspec(0), xspec(1), xspec(2),
                  pl.BlockSpec((1, kblk, DFT_N2, 2 * LANES), lambda c, p, kb: (c, kb, 0, 0)),
                  wspec(0), wspec(1), wspec(2), bspec(0), bspec(1), bspec(2),
                  pl.BlockSpec((2 * n1, n1), full), pl.BlockSpec((n1, 2 * n1), full),
                  sq, sq, tw, tw],
        out_specs=pl.BlockSpec((2, l, LANES), lambda c, p, kb: (p, 0, c)),
        out_shape=jax.ShapeDtypeStruct((b, l, cw), BF16),
        scratch_shapes=[pltpu.VMEM((2, half * DFT_PITCH, LANES), F32),
                        pltpu.VMEM((2, n1 * DFT_PITCH, LANES), F32)],
        compiler_params=_cparams(("arbitrary", "arbitrary", "arbitrary"), vmem),
        name="hyena_conv",
    )(proj, proj, proj, kf, cwp, cwp, cwp, cbp, cbp, cbp,
      tabs["f1c"], tabs["f1i"], tabs["f2r"], tabs["f2i"], tabs["twr"], tabs["twi"])


def _outproj_kernel(of_ref, ob_ref, z_ref, y_ref, x_ref, ga_ref, gn_ref, nw_ref, w_ref, o_ref):
    parts = []
    for h in range(GDN_HEADS):
        hs = slice(h * GDN_HEAD_DIM, (h + 1) * GDN_HEAD_DIM)
        o = of_ref[0, :, hs].astype(F32) + ob_ref[0, :, hs].astype(F32)
        o = _rms_rows(o, gn_ref[...]) * _silu(z_ref[0, :, hs].astype(F32))
        parts.append(o.astype(BF16))
    parts.append(y_ref[0])
    cat = jnp.concatenate(parts, axis=-1)
    out = _dot(cat, w_ref[...])
    o_ref[0] = x_ref[0] + ga_ref[0] * _rms_rows(out, nw_ref[...])


def _out_proj(o_f, o_b, proj, zcol0, y, x, g_a, gdn_norm, norm_w, w_out, tm):
    b, l, d = x.shape
    zb = zcol0 // GDN_W
    row = lambda bi, i: (bi, i, 0)
    vmem = 4 * 2 * tm * GDN_W * 2 + 4 * tm * d * 4 + 2 * (2 * GDN_W) * d * 2 + 6 * tm * d * 4
    return pl.pallas_call(
        _outproj_kernel,
        grid=(b, l // tm),
        in_specs=[pl.BlockSpec((1, tm, GDN_W), row), pl.BlockSpec((1, tm, GDN_W), row),
                  pl.BlockSpec((1, tm, GDN_W), lambda bi, i: (bi, i, zb)),
                  pl.BlockSpec((1, tm, y.shape[2]), row),
                  pl.BlockSpec((1, tm, d), row),
                  pl.BlockSpec((1, 1, d), lambda bi, i: (bi, 0, 0)),
                  pl.BlockSpec((1, GDN_HEAD_DIM), lambda bi, i: (0, 0)),
                  pl.BlockSpec((1, d), lambda bi, i: (0, 0)),
                  pl.BlockSpec(w_out.shape, lambda bi, i: (0, 0))],
        out_specs=pl.BlockSpec((1, tm, d), row),
        out_shape=jax.ShapeDtypeStruct((b, l, d), F32),
        compiler_params=_cparams(("arbitrary", "arbitrary"), vmem),
        name="out_proj",
    )(o_f, o_b, proj, y, x, g_a, gdn_norm.reshape(1, GDN_HEAD_DIM), norm_w.reshape(1, d), w_out)


def _ffn_kernel(s_ref, sh_ref, sc_ref, gf_ref, npre_ref, npost_ref, wg_ref, wu_ref, wd_ref, o_ref, h_ref, acc_ref):
    j = pl.program_id(2)

    @pl.when(j == 0)
    def _():
        h = _rms_rows(s_ref[0], npre_ref[...]) * (1.0 + sc_ref[0]) + sh_ref[0]
        h_ref[...] = h.astype(BF16)
        acc_ref[...] = jnp.zeros_like(acc_ref)

    hb = h_ref[...]
    act = (_silu(_dot(hb, wg_ref[...])) * _dot(hb, wu_ref[...])).astype(BF16)
    acc_ref[...] += _dot(act, wd_ref[...])

    @pl.when(j == pl.num_programs(2) - 1)
    def _():
        o_ref[0] = s_ref[0] + gf_ref[0] * _rms_rows(acc_ref[...], npost_ref[...])


def _ffn(s, sh, sc, g_f, norm_pre, norm_post, w_gate, w_up, w_down, tm, tf):
    b, l, d = s.shape
    f = w_gate.shape[1]
    row = lambda bi, i, j: (bi, i, 0)
    mod = pl.BlockSpec((1, 1, d), lambda bi, i, j: (bi, 0, 0))
    nrm = pl.BlockSpec((1, d), lambda bi, i, j: (0, 0))
    vmem = 4 * tm * d * 4 + tm * d * 4 + tm * d * 2 + 3 * 2 * d * tf * 2 + 4 * tm * tf * 4 + 2 * tm * d * 4
    return pl.pallas_call(
        _ffn_kernel,
        grid=(b, l // tm, f // tf),
        in_specs=[pl.BlockSpec((1, tm, d), row), mod, mod, mod, nrm, nrm,
                  pl.BlockSpec((d, tf), lambda bi, i, j: (0, j)),
                  pl.BlockSpec((d, tf), lambda bi, i, j: (0, j)),
                  pl.BlockSpec((tf, d), lambda bi, i, j: (j, 0))],
        out_specs=pl.BlockSpec((1, tm, d), row),
        out_shape=jax.ShapeDtypeStruct((b, l, d), F32),
        scratch_shapes=[pltpu.VMEM((tm, d), BF16), pltpu.VMEM((tm, d), F32)],
        compiler_params=_cparams(("arbitrary", "arbitrary", "arbitrary"), vmem),
        name="ffn",
    )(s, sh, sc, g_f, norm_pre.reshape(1, d), norm_post.reshape(1, d), w_gate, w_up, w_down)


COL_Z = 3 * GDN_W
COL_HY = 4 * GDN_W


def _split_w_in(w_in):
    d = w_in.shape[0]
    g0 = 3 * GDN_W
    g1 = g0 + 4 * GDN_HEADS
    w_main = jnp.concatenate([w_in[:, :g0].astype(BF16), w_in[:, g1:].astype(BF16)], axis=1)
    w_gate = jnp.zeros((d, LANES), BF16).at[:, :4 * GDN_HEADS].set(w_in[:, g0:g1].astype(BF16))
    return w_main, w_gate


def _row_tile(l, want):
    return want if l % want == 0 else l


def _gdn_branch(x, sh, sc, norm_w, w_main, w_gate, conv_qkv, a_log, dt_bias, s0):
    l = x.shape[1]
    proj, gates = _in_proj(x, sh, sc, norm_w, w_main, w_gate, _row_tile(l, 1024))
    k, v, q, gcol, grow = _gdn_prep(proj, gates, conv_qkv, a_log, dt_bias, _row_tile(l, 256))
    u, w, kd, qg, at, eg = _gdn_chunk(k, v, q, gcol, grow, _row_tile(l, 4 * GDN_CHUNK))
    o_f, o_b, s_fin = _gdn_scan(u, w, kd, qg, at, eg, s0, _row_tile(l, 4 * GDN_CHUNK))
    return proj, o_f, o_b, s_fin


def kernel(x, c, ctx, c_ctx, w_mod, b_mod, norm_pre_mix, norm_post_mix, norm_pre_ffn, norm_post_ffn, w_in, conv_qkv, a_log, dt_bias, gdn_norm, conv_hy, conv_hy_b, filt_w1, filt_b1, filt_freq1, filt_w2, filt_b2, filt_freq2, filt_w3, filt_b3, filt_freq3, filt_w4, hyena_bias, w_out, w_gate, w_up, w_down):
    depth = w_in.shape[0]
    bsz, seq_len, d = x.shape
    assert depth == 1, "the context stream continuation of deeper stacks is not implemented"
    layer = 0
    cvec = jnp.zeros((8, d), F32).at[:bsz].set(c).at[bsz].set(c_ctx)
    mod = _adaln_mod(cvec, w_mod[layer], b_mod[layer])
    sh_a, sc_a, g_a, sh_f, sc_f, g_f = [m.reshape(8, 1, d) for m in jnp.split(mod, 6, axis=-1)]
    ctx_rows = lambda m: jnp.broadcast_to(m[bsz:bsz + 1], (bsz, 1, d))
    w_main, w_gt = _split_w_in(w_in.reshape(w_in.shape[1:]))
    zeros = jnp.zeros((2, bsz, GDN_HEADS, GDN_HEAD_DIM, GDN_HEAD_DIM), F32)
    _, _, _, s_ctx = _gdn_branch(ctx, ctx_rows(sh_a), ctx_rows(sc_a), norm_pre_mix[layer], w_main, w_gt,
                                 conv_qkv[layer], a_log[layer], dt_bias[layer], zeros)
    proj, o_f, o_b, _ = _gdn_branch(x, sh_a[:bsz], sc_a[:bsz], norm_pre_mix[layer], w_main, w_gt,
                                    conv_qkv[layer], a_log[layer], dt_bias[layer], s_ctx)
    tabs = _dft_tables(seq_len)
    hmlp = _hyena_mlp(seq_len, filt_w1[layer], filt_b1[layer], filt_freq1[layer], filt_w2[layer], filt_b2[layer],
                      filt_freq2[layer], filt_w3[layer], filt_b3[layer], filt_freq3[layer])
    kf = _hyena_spec(hmlp, filt_w4[layer], hyena_bias[layer], tabs, seq_len)
    y = _hyena_conv(proj, COL_HY, kf, conv_hy[layer], conv_hy_b[layer], tabs, seq_len)
    as_bf16 = lambda w: w.reshape(w.shape[1:]).astype(BF16)
    s1 = _out_proj(o_f, o_b, proj, COL_Z, y, x, g_a[:bsz], gdn_norm[layer], norm_post_mix[layer],
                   as_bf16(w_out), _row_tile(seq_len, 512))
    return _ffn(s1, sh_f[:bsz], sc_f[:bsz], g_f[:bsz], norm_pre_ffn[layer], norm_post_ffn[layer],
                as_bf16(w_gate), as_bf16(w_up), as_bf16(w_down),
                _row_tile(seq_len, 512), 512)
```

```python
import functools
import math

import numpy as np
import jax
import jax.numpy as jnp
from jax import lax
from jax.experimental import pallas as pl
from jax.experimental.pallas import tpu as pltpu

F32 = jnp.float32
BF16 = jnp.bfloat16

EPS = 1e-6
GDN_HEADS = 8
GDN_HEAD_DIM = 128
GDN_W = GDN_HEADS * GDN_HEAD_DIM
GDN_CONV = 5
GDN_CHUNK = 64
INV_BLOCK = 16
HYENA_CONV = 3
HYENA_EMB = 33
HYENA_DECAY_TARGET = 1e-2
HYENA_FAST_PCT = 0.3
HYENA_SLOW_PCT = 1.5

LANES = 128
BF16_ROWS = 16
DFT_N2 = 128
DFT_PITCH = 136
DFT_KBLK = 16
V7X_SCOPED_VMEM_CAP = 60000 * 1024


def _cparams(sem, vmem_bytes):
    limit = int(min(max(vmem_bytes, 16 * 1024 * 1024), V7X_SCOPED_VMEM_CAP))
    return pltpu.CompilerParams(dimension_semantics=sem, vmem_limit_bytes=limit)


def _silu(x):
    return x * jax.nn.sigmoid(x)


def _softplus(x):
    return jnp.maximum(x, 0.0) + jnp.log1p(jnp.exp(-jnp.abs(x)))


def _rms_rows(x, w):
    return x * lax.rsqrt(jnp.mean(x * x, axis=-1, keepdims=True) + EPS) * w


def _dot(a, b):
    return jnp.dot(a, b, preferred_element_type=F32)


def _dot_nt(a, b):
    return lax.dot_general(a, b, (((1,), (1,)), ((), ())), preferred_element_type=F32)


def _dot_tn(a, b):
    return lax.dot_general(a, b, (((0,), (0,)), ((), ())), preferred_element_type=F32)


def _dot_hi(a, b):
    return jnp.dot(a, b, preferred_element_type=F32, precision=lax.Precision.HIGHEST)


def _mod_kernel(c_ref, w_ref, b_ref, o_ref):
    a = _silu(c_ref[...]).astype(BF16)
    o_ref[...] = _dot(a, w_ref[...].astype(BF16)) + b_ref[...]


def _adaln_mod(cvec, w_mod, b_mod):
    rows, d = cvec.shape
    n = w_mod.shape[1]
    tn = 1024
    return pl.pallas_call(
        _mod_kernel,
        grid=(n // tn,),
        in_specs=[pl.BlockSpec((rows, d), lambda j: (0, 0)),
                  pl.BlockSpec((d, tn), lambda j: (0, j)),
                  pl.BlockSpec((1, tn), lambda j: (0, j))],
        out_specs=pl.BlockSpec((rows, tn), lambda j: (0, j)),
        out_shape=jax.ShapeDtypeStruct((rows, n), F32),
        compiler_params=_cparams(("arbitrary",), 2 * d * tn * 4 + 4 * d * tn),
        name="adaln_mod",
    )(cvec, w_mod, b_mod.reshape(1, n))


def _inproj_kernel(x_ref, sh_ref, sc_ref, nw_ref, w_ref, wg_ref, o_ref, og_ref, h_ref):
    @pl.when(pl.program_id(2) == 0)
    def _():
        h = _rms_rows(x_ref[0], nw_ref[...]) * (1.0 + sc_ref[0]) + sh_ref[0]
        hb = h.astype(BF16)
        h_ref[...] = hb
        og_ref[0] = _dot(hb, wg_ref[...])

    o_ref[0] = _dot(h_ref[...], w_ref[...]).astype(o_ref.dtype)


def _in_proj(x, sh, sc, norm_w, w_main, w_gate, tm):
    b, l, d = x.shape
    n = w_main.shape[1]
    tn = 1024
    vmem = 2 * tm * d * 4 + 2 * d * tn * 2 + 2 * tm * tn * 2 + tm * d * 2 + 4 * d * LANES * 2 + 4 * tm * d * 4
    return pl.pallas_call(
        _inproj_kernel,
        grid=(b, l // tm, n // tn),
        in_specs=[pl.BlockSpec((1, tm, d), lambda bi, i, j: (bi, i, 0)),
                  pl.BlockSpec((1, 1, d), lambda bi, i, j: (bi, 0, 0)),
                  pl.BlockSpec((1, 1, d), lambda bi, i, j: (bi, 0, 0)),
                  pl.BlockSpec((1, d), lambda bi, i, j: (0, 0)),
                  pl.BlockSpec((d, tn), lambda bi, i, j: (0, j)),
                  pl.BlockSpec((d, LANES), lambda bi, i, j: (0, 0))],
        out_specs=[pl.BlockSpec((1, tm, tn), lambda bi, i, j: (bi, i, j)),
                   pl.BlockSpec((1, tm, LANES), lambda bi, i, j: (bi, i, 0))],
        out_shape=[jax.ShapeDtypeStruct((b, l, n), BF16),
                   jax.ShapeDtypeStruct((b, l, LANES), F32)],
        scratch_shapes=[pltpu.VMEM((tm, d), BF16)],
        compiler_params=_cparams(("arbitrary", "arbitrary", "arbitrary"), vmem),
        name="in_proj",
    )(x, sh, sc, norm_w.reshape(1, d), w_main, w_gate)


def _gdn_prep_kernel(p_ref, pp_ref, pn_ref, ba_ref, cw_ref, gp_ref,
                     k_ref, v_ref, q_ref, gc_ref, gr_ref):
    i = pl.program_id(1)
    ts = p_ref.shape[1]
    has_prev = (i > 0).astype(F32)
    has_next = (i < pl.num_programs(1) - 1).astype(F32)
    pad = GDN_CONV // 2
    outs = (k_ref, v_ref, q_ref)
    for t in range(3 * GDN_HEADS):
        cs = t * LANES
        top = pp_ref[0, :, cs:cs + LANES].astype(F32)[BF16_ROWS - 8:] * has_prev
        mid = p_ref[0, :, cs:cs + LANES].astype(F32)
        bot = pn_ref[0, :, cs:cs + LANES].astype(F32)[:8] * has_next
        ext = jnp.concatenate([top, mid, bot], axis=0)
        acc = ext[8 - pad:8 - pad + ts] * cw_ref[0:1, cs:cs + LANES]
        for d in range(1, GDN_CONV):
            acc = acc + ext[8 - pad + d:8 - pad + d + ts] * cw_ref[d:d + 1, cs:cs + LANES]
        y = _silu(acc)
        grp = t // GDN_HEADS
        hs = (t % GDN_HEADS) * LANES
        if grp == 0:
            y = y * lax.rsqrt(jnp.sum(y * y, axis=-1, keepdims=True) + EPS)
        elif grp == 2:
            y = y * (lax.rsqrt(jnp.sum(y * y, axis=-1, keepdims=True) + EPS) * (GDN_HEAD_DIM ** -0.5))
        outs[grp][0, :, hs:hs + LANES] = y.astype(outs[grp].dtype)

    ba = ba_ref[0]
    lane = lax.broadcasted_iota(jnp.int32, ba.shape, 1)
    beta = jax.nn.sigmoid(ba)
    g = -gp_ref[0:1, :] * _softplus(ba + gp_ref[1:2, :])
    g = jnp.where((lane >= 2 * GDN_HEADS) & (lane < 4 * GDN_HEADS), g, 0.0)
    r = lax.broadcasted_iota(jnp.int32, (GDN_CHUNK, GDN_CHUNK), 0)
    c = lax.broadcasted_iota(jnp.int32, (GDN_CHUNK, GDN_CHUNK), 1)
    lower = (r >= c).astype(F32)
    upper = (r <= c).astype(F32)
    lane_c = lax.broadcasted_iota(jnp.int32, (GDN_CHUNK, LANES), 1)
    for ci in range(ts // GDN_CHUNK):
        rows = slice(ci * GDN_CHUNK, (ci + 1) * GDN_CHUNK)
        gch = g[rows]
        cf = _dot_hi(lower, gch)
        cb = _dot_hi(upper, gch)
        packed = jnp.where(lane_c < 2 * GDN_HEADS, beta[rows],
                           jnp.where(lane_c < 3 * GDN_HEADS, cf, cb))
        gc_ref[0, rows, :] = packed
        gr_ref[0, ci] = packed.T[0:4 * GDN_HEADS, :]


def _gdn_prep(proj, gates, conv_qkv, a_log, dt_bias, ts):
    b, l, _ = proj.shape
    w3 = 3 * GDN_W
    hb = BF16_ROWS
    nblk = l // ts
    cw = jnp.zeros((8, w3), F32).at[:GDN_CONV].set(conv_qkv.astype(F32))
    gp = jnp.zeros((8, LANES), F32)
    gp = gp.at[0, 2 * GDN_HEADS:4 * GDN_HEADS].set(jnp.exp(a_log.astype(F32)).reshape(-1))
    gp = gp.at[1, 2 * GDN_HEADS:4 * GDN_HEADS].set(dt_bias.astype(F32).reshape(-1))
    last_hb = l // hb - 1
    out_kvq = jax.ShapeDtypeStruct((b, l, GDN_W), BF16)
    return pl.pallas_call(
        _gdn_prep_kernel,
        grid=(b, nblk),
        in_specs=[pl.BlockSpec((1, ts, w3), lambda bi, i: (bi, i, 0)),
                  pl.BlockSpec((1, hb, w3), lambda bi, i: (bi, jnp.maximum(i * (ts // hb) - 1, 0), 0)),
                  pl.BlockSpec((1, hb, w3), lambda bi, i: (bi, jnp.minimum((i + 1) * (ts // hb), last_hb), 0)),
                  pl.BlockSpec((1, ts, LANES), lambda bi, i: (bi, i, 0)),
                  pl.BlockSpec((8, w3), lambda bi, i: (0, 0)),
                  pl.BlockSpec((8, LANES), lambda bi, i: (0, 0))],
        out_specs=[pl.BlockSpec((1, ts, GDN_W), lambda bi, i: (bi, i, 0)),
                   pl.BlockSpec((1, ts, GDN_W), lambda bi, i: (bi, i, 0)),
                   pl.BlockSpec((1, ts, GDN_W), lambda bi, i: (bi, i, 0)),
                   pl.BlockSpec((1, ts, LANES), lambda bi, i: (bi, i, 0)),
                   pl.BlockSpec((1, ts // GDN_CHUNK, 4 * GDN_HEADS, GDN_CHUNK), lambda bi, i: (bi, i, 0, 0))],
        out_shape=[out_kvq, out_kvq, out_kvq,
                   jax.ShapeDtypeStruct((b, l, LANES), F32),
                   jax.ShapeDtypeStruct((b, l // GDN_CHUNK, 4 * GDN_HEADS, GDN_CHUNK), F32)],
        compiler_params=_cparams(("arbitrary", "arbitrary"), 32 * 1024 * 1024),
        name="gdn_prep",
    )(proj, proj, proj, gates, cw, gp)


def _bmm(a, b):
    return jnp.einsum('gmk,gkn->gmn', a, b, preferred_element_type=F32)


def _bmm_nt(a, b):
    return jnp.einsum('gmk,gnk->gmn', a, b, preferred_element_type=F32)


def _bmm_tn(a, b):
    return jnp.einsum('gkm,gkn->gmn', a, b, preferred_element_type=F32)


def _tri_inverse(a, blockdiag):
    n = a.shape[-1]
    r = lax.broadcasted_iota(jnp.int32, (n, n), 0)
    c = lax.broadcasted_iota(jnp.int32, (n, n), 1)
    eye = (r == c).astype(F32)
    d = jnp.where(blockdiag, a, 0.0)
    e = a - d
    db = d.astype(BF16)
    d2b = _bmm(db, db).astype(BF16)
    d4b = _bmm(d2b, d2b).astype(BF16)
    d8b = _bmm(d4b, d4b).astype(BF16)
    x = eye - d
    x = x + _bmm(x.astype(BF16), d2b)
    x = x + _bmm(x.astype(BF16), d4b)
    tdiag = x + _bmm(x.astype(BF16), d8b)
    tdb = tdiag.astype(BF16)
    nb = _bmm(tdb, e.astype(BF16)).astype(BF16)
    n2b = _bmm(nb, nb).astype(BF16)
    y = tdiag + _bmm(n2b, tdb)
    return y - _bmm(nb, y.astype(BF16))


def _gdn_chunk_kernel(k_ref, v_ref, q_ref, gc_ref, gr_ref, u_ref, w_ref, kd_ref, qg_ref, at_ref, eg_ref):
    n = GDN_CHUNK
    nch = k_ref.shape[1] // n
    r = lax.broadcasted_iota(jnp.int32, (n, n), 0)
    c = lax.broadcasted_iota(jnp.int32, (n, n), 1)
    blockdiag = (r // INV_BLOCK) == (c // INV_BLOCK)
    strict = (r > c, r < c)
    incl = (r >= c, r <= c)
    pairs = [(ci, h) for ci in range(nch) for h in range(GDN_HEADS)]
    np_ = len(pairs)

    def tile(ref, ci, h):
        return ref[0, ci * n:(ci + 1) * n, h * GDN_HEAD_DIM:(h + 1) * GDN_HEAD_DIM]

    ks = jnp.stack([tile(k_ref, ci, h) for ci, h in pairs])
    vs = jnp.stack([tile(v_ref, ci, h) for ci, h in pairs])
    qs = jnp.stack([tile(q_ref, ci, h) for ci, h in pairs])
    kk = _bmm_nt(ks, ks)
    qk = _bmm_nt(qs, ks)
    gcol = gc_ref[0]
    a_parts, at_parts, gcb, brow, grow = [], [], [], [], []
    for d in range(2):
        off = d * GDN_HEADS
        cols = lambda lane: jnp.stack([jnp.broadcast_to(gcol[ci * n:(ci + 1) * n, lane + h:lane + h + 1], (n, LANES))
                                       for ci, h in pairs])
        rows = lambda row: jnp.stack([gr_ref[0, ci, row + h:row + h + 1, :] for ci, h in pairs])
        gcb_d = cols(2 * GDN_HEADS + off)
        beta_c = cols(off)[:, :, :n]
        gc_r = rows(2 * GDN_HEADS + off)
        dec = jnp.exp(jnp.where(incl[d], gcb_d[:, :, :n] - gc_r, 0.0))
        a_parts.append(jnp.where(strict[d], beta_c * kk * dec, 0.0))
        at_parts.append(jnp.where(incl[d], qk * dec, 0.0).astype(BF16))
        gcb.append(gcb_d)
        brow.append(rows(off))
        grow.append(gc_r)
    t = _tri_inverse(jnp.concatenate(a_parts, axis=0), blockdiag)
    tb = t * jnp.concatenate(brow, axis=0)
    k2 = jnp.concatenate([ks, ks], axis=0)
    u = _bmm(tb.astype(BF16), jnp.concatenate([vs, vs], axis=0))
    w = _bmm((tb * jnp.exp(jnp.concatenate(grow, axis=0))).astype(BF16), k2)
    kf = ks.astype(F32)
    qf = qs.astype(F32)
    for d in range(2):
        last = 0 if d else n - 1
        g_last = gcb[d][:, last:last + 1, :]
        kd = (kf * jnp.exp(g_last - gcb[d])).astype(BF16)
        qg = (qf * jnp.exp(gcb[d])).astype(BF16)
        eg = jnp.exp(g_last)
        for p, (ci, h) in enumerate(pairs):
            rs = slice(ci * n, (ci + 1) * n)
            hs = slice(h * GDN_HEAD_DIM, (h + 1) * GDN_HEAD_DIM)
            u_ref[d, 0, rs, hs] = u[d * np_ + p]
            w_ref[d, 0, rs, hs] = w[d * np_ + p].astype(BF16)
            kd_ref[d, 0, rs, hs] = kd[p]
            qg_ref[d, 0, rs, hs] = qg[p]
            at_ref[d, 0, h, rs, :] = at_parts[d][p]
            eg_ref[d, 0, ci, h:h + 1, :] = eg[p]


def _gdn_chunk(k, v, q, gcol, grow, ts):
    b, l, _ = k.shape
    nc = l // GDN_CHUNK
    row = lambda bi, i: (bi, i, 0)
    orow = lambda bi, i: (0, bi, i, 0)
    wide = pl.BlockSpec((2, 1, ts, GDN_W), orow)
    shp = lambda dt: jax.ShapeDtypeStruct((2, b, l, GDN_W), dt)
    return pl.pallas_call(
        _gdn_chunk_kernel,
        grid=(b, l // ts),
        in_specs=[pl.BlockSpec((1, ts, GDN_W), row), pl.BlockSpec((1, ts, GDN_W), row),
                  pl.BlockSpec((1, ts, GDN_W), row), pl.BlockSpec((1, ts, LANES), row),
                  pl.BlockSpec((1, ts // GDN_CHUNK, 4 * GDN_HEADS, GDN_CHUNK), lambda bi, i: (bi, i, 0, 0))],
        out_specs=[wide, wide, wide, wide,
                   pl.BlockSpec((2, 1, GDN_HEADS, ts, GDN_CHUNK), lambda bi, i: (0, bi, 0, i, 0)),
                   pl.BlockSpec((2, 1, ts // GDN_CHUNK, GDN_HEADS, LANES), lambda bi, i: (0, bi, i, 0, 0))],
        out_shape=[shp(F32), shp(BF16), shp(BF16), shp(BF16),
                   jax.ShapeDtypeStruct((2, b, GDN_HEADS, l, GDN_CHUNK), BF16),
                   jax.ShapeDtypeStruct((2, b, nc, GDN_HEADS, LANES), F32)],
        compiler_params=_cparams(("arbitrary", "arbitrary"), 32 * 1024 * 1024),
        name="gdn_chunk",
    )(k, v, q, gcol, grow)


def _gdn_scan_kernel(uf, wf, kdf, qgf, atf, egf, ub, wb, kdb, qgb, atb, egb, s0_ref,
                     of_ref, ob_ref, sfin_ref, s_ref):
    i = pl.program_id(1)
    n = GDN_CHUNK
    nch = uf.shape[2] // n
    ng = 2 * GDN_HEADS

    @pl.when(i == 0)
    def _():
        s_ref[...] = s0_ref[:, 0].reshape(ng, GDN_HEAD_DIM, GDN_HEAD_DIM)

    refs = ((uf, wf, kdf, qgf, atf, egf, of_ref), (ub, wb, kdb, qgb, atb, egb, ob_ref))
    for step in range(nch):
        cis = (step, nch - 1 - step)

        def gather(idx, head_major=False):
            out = []
            for d in range(2):
                rs = slice(cis[d] * n, (cis[d] + 1) * n)
                for h in range(GDN_HEADS):
                    if head_major:
                        out.append(refs[d][idx][0, 0, h, rs, :])
                    else:
                        out.append(refs[d][idx][0, 0, rs, h * GDN_HEAD_DIM:(h + 1) * GDN_HEAD_DIM])
            return jnp.stack(out)

        u, w, kd, qg = gather(0), gather(1), gather(2), gather(3)
        at = gather(4, head_major=True)
        eg = jnp.stack([refs[d][5][0, 0, cis[d], h:h + 1, :] for d in range(2) for h in range(GDN_HEADS)])
        s = s_ref[...]
        sb = s.astype(BF16)
        v_new = (u - _bmm(w, sb)).astype(BF16)
        o = _bmm(qg, sb) + _bmm(at, v_new)
        s_ref[...] = s * eg + _bmm_tn(kd, v_new)
        for d in range(2):
            rs = slice(cis[d] * n, (cis[d] + 1) * n)
            for h in range(GDN_HEADS):
                refs[d][6][0, rs, h * GDN_HEAD_DIM:(h + 1) * GDN_HEAD_DIM] = o[d * GDN_HEADS + h].astype(BF16)

    @pl.when(i == pl.num_programs(1) - 1)
    def _():
        sfin_ref[:, 0] = s_ref[...].reshape(2, GDN_HEADS, GDN_HEAD_DIM, GDN_HEAD_DIM)


def _gdn_scan(u, w, kd, qg, at, eg, s0, ts):
    _, b, l, _ = u.shape
    nb = l // ts
    nch = ts // GDN_CHUNK

    def specs(d):
        blk = (lambda i: i) if d == 0 else (lambda i: nb - 1 - i)
        wide = pl.BlockSpec((1, 1, ts, GDN_W), lambda bi, i: (d, bi, blk(i), 0))
        return [wide, wide, wide, wide,
                pl.BlockSpec((1, 1, GDN_HEADS, ts, GDN_CHUNK), lambda bi, i: (d, bi, 0, blk(i), 0)),
                pl.BlockSpec((1, 1, nch, GDN_HEADS, LANES), lambda bi, i: (d, bi, blk(i), 0, 0))]

    sblk = (2, 1, GDN_HEADS, GDN_HEAD_DIM, GDN_HEAD_DIM)
    smap = lambda bi, i: (0, bi, 0, 0, 0)
    oshape = jax.ShapeDtypeStruct((b, l, GDN_W), BF16)
    args = (u, w, kd, qg, at, eg)
    return pl.pallas_call(
        _gdn_scan_kernel,
        grid=(b, nb),
        in_specs=specs(0) + specs(1) + [pl.BlockSpec(sblk, smap)],
        out_specs=[pl.BlockSpec((1, ts, GDN_W), lambda bi, i: (bi, i, 0)),
                   pl.BlockSpec((1, ts, GDN_W), lambda bi, i: (bi, nb - 1 - i, 0)),
                   pl.BlockSpec(sblk, smap)],
        out_shape=[oshape, oshape,
                   jax.ShapeDtypeStruct((2, b, GDN_HEADS, GDN_HEAD_DIM, GDN_HEAD_DIM), F32)],
        scratch_shapes=[pltpu.VMEM((2 * GDN_HEADS, GDN_HEAD_DIM, GDN_HEAD_DIM), F32)],
        compiler_params=_cparams(("arbitrary", "arbitrary"), 32 * 1024 * 1024),
        name="gdn_scan",
    )(*args, *args, s0)


def _hyena_mlp_kernel(fr_ref, w1_ref, b1_ref, f1_ref, w2_ref, b2_ref, f2_ref, w3_ref, b3_ref, f3_ref,
                      o_ref, *, seq_len):
    tl = o_ref.shape[0]
    i = pl.program_id(0)
    row = (lax.broadcasted_iota(jnp.int32, (tl, LANES), 0) + i * tl).astype(F32)
    lane = lax.broadcasted_iota(jnp.int32, (tl, LANES), 1)
    bands = (HYENA_EMB - 1) // 2
    t01 = row / max(seq_len - 1, 1)
    ang = (2.0 * math.pi / seq_len) * row * fr_ref[...]
    z = jnp.where(lane == 0, t01,
                  jnp.where(lane <= bands, jnp.cos(ang),
                            jnp.where(lane <= 2 * bands, -jnp.sin(ang), 0.0)))
    h = jnp.sin(f1_ref[...] * (_dot_hi(z, w1_ref[...]) + b1_ref[...]))
    h = jnp.sin(f2_ref[...] * (_dot_hi(h, w2_ref[...]) + b2_ref[...]))
    h = jnp.sin(f3_ref[...] * (_dot_hi(h, w3_ref[...]) + b3_ref[...]))
    o_ref[...] = h


def _hyena_mlp(seq_len, w1, b1, f1, w2, b2, f2, w3, b3, f3):
    fw = w2.shape[0]
    bands = (HYENA_EMB - 1) // 2
    freqs = np.linspace(1e-4, bands - 1, bands, dtype=np.float32)
    fr = np.zeros((1, LANES), np.float32)
    fr[0, 1:1 + bands] = freqs
    fr[0, 1 + bands:1 + 2 * bands] = freqs
    w1p = jnp.zeros((LANES, fw), F32).at[:HYENA_EMB].set(w1.astype(F32))
    tl = min(seq_len, 1024)
    full = lambda i: (0, 0)
    vec = pl.BlockSpec((1, fw), full)
    mat = pl.BlockSpec((fw, fw), full)
    r2 = lambda a: a.astype(F32).reshape(1, fw)
    return pl.pallas_call(
        functools.partial(_hyena_mlp_kernel, seq_len=seq_len),
        grid=(seq_len // tl,),
        in_specs=[pl.BlockSpec((1, LANES), full), pl.BlockSpec((LANES, fw), full), vec, vec,
                  mat, vec, vec, mat, vec, vec],
        out_specs=pl.BlockSpec((tl, fw), lambda i: (i, 0)),
        out_shape=jax.ShapeDtypeStruct((seq_len, fw), F32),
        compiler_params=_cparams(("arbitrary",), 16 * 1024 * 1024),
        name="hyena_mlp",
    )(jnp.asarray(fr), w1p, r2(b1), r2(f1), w2.astype(F32), r2(b2), r2(f2), w3.astype(F32), r2(b3), r2(f3))


def _dft_tables(seq_len):
    n1 = 2 * seq_len // DFT_N2
    n = n1 * DFT_N2
    half = n1 // 2
    k1 = np.arange(n1, dtype=np.float64)[:, None]
    a1 = 2.0 * np.pi * k1 * np.arange(n1, dtype=np.float64)[None, :] / n1
    c1, s1 = np.cos(a1), np.sin(a1)
    f1c = np.block([[c1[:, :half], s1[:, :half]], [-s1[:, :half], c1[:, :half]]])
    f1r = np.concatenate([c1[:, :half], -s1[:, :half]], axis=0)
    ct, st = c1.T[:half], s1.T[:half]
    f1i = np.block([[ct, -st], [st, ct]]) / n
    a2 = 2.0 * np.pi * np.arange(DFT_N2, dtype=np.float64)[:, None] * np.arange(DFT_N2, dtype=np.float64)[None, :] / DFT_N2
    atw = 2.0 * np.pi * k1 * np.arange(DFT_N2, dtype=np.float64)[None, :] / n
    as32 = lambda a: jnp.asarray(a.astype(np.float32))
    return dict(n1=n1, f1c=as32(f1c), f1r=as32(f1r), f1i=as32(f1i),
                f2r=as32(np.cos(a2)), f2i=as32(-np.sin(a2)),
                twr=as32(np.cos(atw)), twi=as32(-np.sin(atw)))


def _twiddled_dft(f2r, f2i, twr_row, twi_row):
    return f2r * twr_row - f2i * twi_row, f2r * twi_row + f2i * twr_row


def _split_bf16(x):
    hi = x.astype(BF16)
    return hi, (x - hi.astype(F32)).astype(BF16)


def _dot2(a, b):
    ah = a.astype(BF16)
    bh, bl = _split_bf16(b)
    return _dot(ah, bh) + _dot(ah, bl)


def _hyena_spec_kernel(h_ref, w4f_ref, w4b_ref, dl_ref, bias_ref, f1r_ref, f2r_ref, f2i_ref, twr_ref, twi_ref,
                       o_ref, taps_ref, a_ref, *, seq_len, kblk):
    n1 = twr_ref.shape[0]
    half = n1 // 2
    kb = pl.program_id(1)
    q0, q1, q2, q3 = (slice(i * LANES, (i + 1) * LANES) for i in range(4))

    @pl.when(kb == 0)
    def _():
        def taps_block(blk, carry):
            r0 = pl.multiple_of(blk * DFT_N2, DFT_N2)
            t = (lax.broadcasted_iota(jnp.int32, (DFT_N2, LANES), 0) + r0).astype(F32)
            win = jnp.exp(-(t / max(seq_len - 1, 1)) * dl_ref[...])
            d0 = pl.multiple_of(blk * DFT_PITCH, 8)
            hb = h_ref[pl.ds(r0, DFT_N2), :]
            taps_ref[0, pl.ds(d0, DFT_N2), :] = (_dot_hi(hb, w4f_ref[...]) * win
                                                 + jnp.where(t == 0.0, bias_ref[...], 0.0))
            taps_ref[1, pl.ds(d0, DFT_N2), :] = _dot_hi(hb, w4b_ref[...]) * jnp.where(t == 0.0, 0.0, win)
            return carry

        lax.fori_loop(0, half, taps_block, 0)
        f1 = f1r_ref[...]

        def stage1(n2, carry):
            x = jnp.concatenate([taps_ref[0, pl.ds(n2, half, stride=DFT_PITCH), :],
                                 taps_ref[1, pl.ds(n2, half, stride=DFT_PITCH), :]], axis=1)
            a = _dot2(f1, x)
            a_ref[0, pl.ds(n2, n1, stride=DFT_PITCH), :] = a[:n1, q0]
            a_ref[1, pl.ds(n2, n1, stride=DFT_PITCH), :] = a[:n1, q1]
            a_ref[2, pl.ds(n2, n1, stride=DFT_PITCH), :] = a[n1:, q0]
            a_ref[3, pl.ds(n2, n1, stride=DFT_PITCH), :] = a[n1:, q1]
            return carry

        lax.fori_loop(0, DFT_N2, stage1, 0, unroll=8)

    f2r = f2r_ref[...]
    f2i = f2i_ref[...]

    def stage2(j, carry):
        k1 = kb * kblk + j
        gr, gi = _twiddled_dft(f2r, f2i, twr_ref[pl.ds(k1, 1), :], twi_ref[pl.ds(k1, 1), :])
        r0 = pl.multiple_of(k1 * DFT_PITCH, 8)
        acat = jnp.concatenate([a_ref[s, pl.ds(r0, DFT_N2), :] for s in range(4)], axis=1)
        r = _dot(jnp.concatenate([gr, gi], axis=0).astype(BF16), acat.astype(BF16))
        top, bot = r[:DFT_N2], r[DFT_N2:]
        ff_r, ff_i = top[:, q0] - bot[:, q2], top[:, q2] + bot[:, q0]
        fb_r, fb_i = top[:, q1] - bot[:, q3], top[:, q3] + bot[:, q1]
        o_ref[0, j, :, q0] = ff_r + fb_r
        o_ref[0, j, :, q1] = ff_i - fb_i
        return carry

    lax.fori_loop(0, kblk, stage2, 0, unroll=8)


def _hyena_spec(hmlp, w4, hyena_bias, tabs, seq_len):
    n1 = tabs["n1"]
    half = n1 // 2
    fw = hmlp.shape[1]
    cw = w4.shape[1] // 2
    ct = cw // LANES
    kblk = min(DFT_KBLK, n1)
    max_decay = math.log(HYENA_DECAY_TARGET) / HYENA_FAST_PCT
    min_decay = math.log(HYENA_DECAY_TARGET) / HYENA_SLOW_PCT
    deltas = jnp.asarray(np.abs(np.linspace(min_decay, max_decay, cw, dtype=np.float32)).reshape(1, cw))
    full = lambda c, kb: (0, 0)
    sq = pl.BlockSpec((DFT_N2, DFT_N2), full)
    tw = pl.BlockSpec((n1, DFT_N2), full)
    vmem = (2 * seq_len * LANES * 4 + half * DFT_PITCH * 2 * LANES * 4 + n1 * DFT_PITCH * 4 * LANES * 4
            + 2 * kblk * DFT_N2 * 2 * LANES * 4 + 8 * 1024 * 1024)
    w4 = w4.astype(F32)
    return pl.pallas_call(
        functools.partial(_hyena_spec_kernel, seq_len=seq_len, kblk=kblk),
        grid=(ct, n1 // kblk),
        in_specs=[pl.BlockSpec((seq_len, fw), full),
                  pl.BlockSpec((fw, LANES), lambda c, kb: (0, c)),
                  pl.BlockSpec((fw, LANES), lambda c, kb: (0, ct + c)),
                  pl.BlockSpec((1, LANES), lambda c, kb: (0, c)),
                  pl.BlockSpec((1, LANES), lambda c, kb: (0, c)),
                  pl.BlockSpec((2 * n1, half), full), sq, sq, tw, tw],
        out_specs=pl.BlockSpec((1, kblk, DFT_N2, 2 * LANES), lambda c, kb: (c, kb, 0, 0)),
        out_shape=jax.ShapeDtypeStruct((ct, n1, DFT_N2, 2 * LANES), F32),
        scratch_shapes=[pltpu.VMEM((2, half * DFT_PITCH, LANES), F32),
                        pltpu.VMEM((4, n1 * DFT_PITCH, LANES), F32)],
        compiler_params=_cparams(("arbitrary", "arbitrary"), vmem),
        name="hyena_spec",
    )(hmlp, w4, w4, deltas, hyena_bias.astype(F32).reshape(1, cw),
      tabs["f1r"], tabs["f2r"], tabs["f2i"], tabs["twr"], tabs["twi"])


def _conv3_block(ref, bi, r0, nrows, seq_len, w_ref, b_ref):
    lo = pl.multiple_of(jnp.maximum(r0 - BF16_ROWS, 0), BF16_ROWS)
    hi = pl.multiple_of(jnp.minimum(r0 + nrows, seq_len - BF16_ROWS), BF16_ROWS)
    top = ref[bi, pl.ds(lo, BF16_ROWS), :].astype(F32)[BF16_ROWS - 8:] * jnp.where(r0 > 0, 1.0, 0.0)
    mid = ref[bi, pl.ds(pl.multiple_of(r0, BF16_ROWS), nrows), :].astype(F32)
    bot = ref[bi, pl.ds(hi, BF16_ROWS), :].astype(F32)[:8] * jnp.where(r0 + nrows < seq_len, 1.0, 0.0)
    ext = jnp.concatenate([top, mid, bot], axis=0)
    return (ext[7:7 + nrows] * w_ref[0:1, :] + ext[8:8 + nrows] * w_ref[1:2, :]
            + ext[9:9 + nrows] * w_ref[2:3, :] + b_ref[...])


def _hyena_conv_kernel(x0_ref, x1_ref, xv_ref, kf_ref, w0_ref, w1_ref, wv_ref, b0_ref, b1_ref, bv_ref,
                       f1c_ref, f1i_ref, f2r_ref, f2i_ref, twr_ref, twi_ref,
                       o_ref, z_ref, a_ref, *, seq_len, kblk):
    n1 = twr_ref.shape[0]
    half = n1 // 2
    kb = pl.program_id(2)
    nkb = pl.num_programs(2)

    def uh_block(bi, r0):
        return (_conv3_block(x1_ref, bi, r0, DFT_N2, seq_len, w1_ref, b1_ref)
                * _conv3_block(xv_ref, bi, r0, DFT_N2, seq_len, wv_ref, bv_ref))

    @pl.when(kb == 0)
    def _():
        def fill(blk, carry):
            r0 = blk * DFT_N2
            d0 = pl.multiple_of(blk * DFT_PITCH, 8)
            z_ref[0, pl.ds(d0, DFT_N2), :] = uh_block(0, r0)
            z_ref[1, pl.ds(d0, DFT_N2), :] = uh_block(1, r0)
            return carry

        lax.fori_loop(0, half, fill, 0)
        f1 = f1c_ref[...].astype(BF16)

        def stage1(n2, carry):
            ur = z_ref[0, pl.ds(n2, half, stride=DFT_PITCH), :]
            ui = z_ref[1, pl.ds(n2, half, stride=DFT_PITCH), :]
            a = _dot(f1, jnp.concatenate([ur, ui], axis=0).astype(BF16))
            a_ref[0, pl.ds(n2, n1, stride=DFT_PITCH), :] = a[:n1]
            a_ref[1, pl.ds(n2, n1, stride=DFT_PITCH), :] = a[n1:]
            return carry

        lax.fori_loop(0, DFT_N2, stage1, 0, unroll=8)

    f2r = f2r_ref[...]
    f2i = f2i_ref[...]
    lo, hi = slice(0, DFT_N2), slice(DFT_N2, 2 * DFT_N2)

    def stage2(j, carry):
        k1 = kb * kblk + j
        gr, gi = _twiddled_dft(f2r, f2i, twr_ref[pl.ds(k1, 1), :], twi_ref[pl.ds(k1, 1), :])
        r0 = pl.multiple_of(k1 * DFT_PITCH, 8)
        acat = jnp.concatenate([a_ref[0, pl.ds(r0, DFT_N2), :], a_ref[1, pl.ds(r0, DFT_N2), :]], axis=1)
        r = _dot(jnp.concatenate([gr, gi], axis=0).astype(BF16), acat.astype(BF16))
        xr = r[lo, lo] - r[hi, hi]
        xi = r[lo, hi] + r[hi, lo]
        kr = kf_ref[0, j, :, lo]
        ki = kf_ref[0, j, :, hi]
        pcat = jnp.concatenate([xr * kr - xi * ki, xr * ki + xi * kr], axis=1).astype(BF16)
        q = _dot_tn(jnp.concatenate([gr, gi], axis=1).astype(BF16), pcat)
        a_ref[0, pl.ds(r0, DFT_N2), :] = q[lo, lo] + q[hi, hi]
        a_ref[1, pl.ds(r0, DFT_N2), :] = q[lo, hi] - q[hi, lo]
        return carry

    lax.fori_loop(0, kblk, stage2, 0, unroll=16)

    @pl.when(kb == nkb - 1)
    def _():
        f1 = f1i_ref[...].astype(BF16)

        def stage_last(n2, carry):
            br = a_ref[0, pl.ds(n2, n1, stride=DFT_PITCH), :]
            bi = a_ref[1, pl.ds(n2, n1, stride=DFT_PITCH), :]
            y = _dot(f1, jnp.concatenate([br, bi], axis=0).astype(BF16))
            z_ref[0, pl.ds(n2, half, stride=DFT_PITCH), :] = y[:half]
            z_ref[1, pl.ds(n2, half, stride=DFT_PITCH), :] = y[half:]
            return carry

        lax.fori_loop(0, DFT_N2, stage_last, 0, unroll=16)

        def emit(blk, carry):
            r0 = blk * DFT_N2
            d0 = pl.multiple_of(blk * DFT_PITCH, 8)
            for bi in range(2):
                x0 = _conv3_block(x0_ref, bi, r0, DFT_N2, seq_len, w0_ref, b0_ref)
                y = x0 * z_ref[bi, pl.ds(d0, DFT_N2), :]
                o_ref[bi, pl.ds(pl.multiple_of(r0, DFT_N2), DFT_N2), :] = y.astype(o_ref.dtype)
            return carry

        lax.fori_loop(0, half, emit, 0)


def _hyena_conv(proj, col0, kf, conv_hy, conv_hy_b, tabs, seq_len):
    b, l, _ = proj.shape
    n1 = tabs["n1"]
    half = n1 // 2
    cw = conv_hy.shape[1] // 3
    ct = cw // LANES
    kblk = min(DFT_KBLK, n1)
    cb0 = col0 // LANES
    cwp = jnp.zeros((8, 3 * cw), F32).at[:HYENA_CONV].set(conv_hy.astype(F32))
    cbp = conv_hy_b.astype(F32).reshape(1, 3 * cw)
    full = lambda c, p, kb: (0, 0)

    def xspec(g):
        return pl.BlockSpec((2, l, LANES), lambda c, p, kb: (p, 0, cb0 + g * ct + c),
                            pipeline_mode=pl.Buffered(1))

    def wspec(g):
        return pl.BlockSpec((8, LANES), lambda c, p, kb: (0, g * ct + c))

    def bspec(g):
        return pl.BlockSpec((1, LANES), lambda c, p, kb: (0, g * ct + c))

    sq = pl.BlockSpec((DFT_N2, DFT_N2), full)
    tw = pl.BlockSpec((n1, DFT_N2), full)
    vmem = (3 * 2 * l * LANES * 2 + 2 * 2 * l * LANES * 2 + 2 * half * DFT_PITCH * LANES * 4
            + 2 * n1 * DFT_PITCH * LANES * 4 + 2 * kblk * DFT_N2 * 2 * LANES * 4 + 6 * 1024 * 1024)
    return pl.pallas_call(
        functools.partial(_hyena_conv_kernel, seq_len=seq_len, kblk=kblk),
        grid=(ct, b // 2, n1 // kblk),
        in_specs=[xspec(0), xspec(1), xspec(2),
                  pl.BlockSpec((1, kblk, DFT_N2, 2 * LANES), lambda c, p, kb: (c, kb, 0, 0)),
                  wspec(0), wspec(1), wspec(2), bspec(0), bspec(1), bspec(2),
                  pl.BlockSpec((2 * n1, n1), full), pl.BlockSpec((n1, 2 * n1), full),
                  sq, sq, tw, tw],
        out_specs=pl.BlockSpec((2, l, LANES), lambda c, p, kb: (p, 0, c)),
        out_shape=jax.ShapeDtypeStruct((b, l, cw), BF16),
        scratch_shapes=[pltpu.VMEM((2, half * DFT_PITCH, LANES), F32),
                        pltpu.VMEM((2, n1 * DFT_PITCH, LANES), F32)],
        compiler_params=_cparams(("arbitrary", "arbitrary", "arbitrary"), vmem),
        name="hyena_conv",
    )(proj, proj, proj, kf, cwp, cwp, cwp, cbp, cbp, cbp,
      tabs["f1c"], tabs["f1i"], tabs["f2r"], tabs["f2i"], tabs["twr"], tabs["twi"])


def _outproj_kernel(of_ref, ob_ref, z_ref, y_ref, x_ref, ga_ref, gn_ref, nw_ref, w_ref, o_ref):
    parts = []
    for h in range(GDN_HEADS):
        hs = slice(h * GDN_HEAD_DIM, (h + 1) * GDN_HEAD_DIM)
        o = of_ref[0, :, hs].astype(F32) + ob_ref[0, :, hs].astype(F32)
        o = _rms_rows(o, gn_ref[...]) * _silu(z_ref[0, :, hs].astype(F32))
        parts.append(o.astype(BF16))
    parts.append(y_ref[0])
    cat = jnp.concatenate(parts, axis=-1)
    out = _dot(cat, w_ref[...])
    o_ref[0] = x_ref[0] + ga_ref[0] * _rms_rows(out, nw_ref[...])


def _out_proj(o_f, o_b, proj, zcol0, y, x, g_a, gdn_norm, norm_w, w_out, tm):
    b, l, d = x.shape
    zb = zcol0 // GDN_W
    row = lambda bi, i: (bi, i, 0)
    vmem = 4 * 2 * tm * GDN_W * 2 + 4 * tm * d * 4 + 2 * (2 * GDN_W) * d * 2 + 6 * tm * d * 4
    return pl.pallas_call(
        _outproj_kernel,
        grid=(b, l // tm),
        in_specs=[pl.BlockSpec((1, tm, GDN_W), row), pl.BlockSpec((1, tm, GDN_W), row),
                  pl.BlockSpec((1, tm, GDN_W), lambda bi, i: (bi, i, zb)),
                  pl.BlockSpec((1, tm, y.shape[2]), row),
                  pl.BlockSpec((1, tm, d), row),
                  pl.BlockSpec((1, 1, d), lambda bi, i: (bi, 0, 0)),
                  pl.BlockSpec((1, GDN_HEAD_DIM), lambda bi, i: (0, 0)),
                  pl.BlockSpec((1, d), lambda bi, i: (0, 0)),
                  pl.BlockSpec(w_out.shape, lambda bi, i: (0, 0))],
        out_specs=pl.BlockSpec((1, tm, d), row),
        out_shape=jax.ShapeDtypeStruct((b, l, d), F32),
        compiler_params=_cparams(("arbitrary", "arbitrary"), vmem),
        name="out_proj",
    )(o_f, o_b, proj, y, x, g_a, gdn_norm.reshape(1, GDN_HEAD_DIM), norm_w.reshape(1, d), w_out)


def _ffn_kernel(s_ref, sh_ref, sc_ref, gf_ref, npre_ref, npost_ref, wg_ref, wu_ref, wd_ref, o_ref, h_ref, acc_ref):
    j = pl.program_id(2)

    @pl.when(j == 0)
    def _():
        h = _rms_rows(s_ref[0], npre_ref[...]) * (1.0 + sc_ref[0]) + sh_ref[0]
        h_ref[...] = h.astype(BF16)
        acc_ref[...] = jnp.zeros_like(acc_ref)

    hb = h_ref[...]
    act = (_silu(_dot(hb, wg_ref[...])) * _dot(hb, wu_ref[...])).astype(BF16)
    acc_ref[...] += _dot(act, wd_ref[...])

    @pl.when(j == pl.num_programs(2) - 1)
    def _():
        o_ref[0] = s_ref[0] + gf_ref[0] * _rms_rows(acc_ref[...], npost_ref[...])


def _ffn(s, sh, sc, g_f, norm_pre, norm_post, w_gate, w_up, w_down, tm, tf):
    b, l, d = s.shape
    f = w_gate.shape[1]
    row = lambda bi, i, j: (bi, i, 0)
    mod = pl.BlockSpec((1, 1, d), lambda bi, i, j: (bi, 0, 0))
    nrm = pl.BlockSpec((1, d), lambda bi, i, j: (0, 0))
    vmem = 4 * tm * d * 4 + tm * d * 4 + tm * d * 2 + 3 * 2 * d * tf * 2 + 4 * tm * tf * 4 + 2 * tm * d * 4
    return pl.pallas_call(
        _ffn_kernel,
        grid=(b, l // tm, f // tf),
        in_specs=[pl.BlockSpec((1, tm, d), row), mod, mod, mod, nrm, nrm,
                  pl.BlockSpec((d, tf), lambda bi, i, j: (0, j)),
                  pl.BlockSpec((d, tf), lambda bi, i, j: (0, j)),
                  pl.BlockSpec((tf, d), lambda bi, i, j: (j, 0))],
        out_specs=pl.BlockSpec((1, tm, d), row),
        out_shape=jax.ShapeDtypeStruct((b, l, d), F32),
        scratch_shapes=[pltpu.VMEM((tm, d), BF16), pltpu.VMEM((tm, d), F32)],
        compiler_params=_cparams(("arbitrary", "arbitrary", "arbitrary"), vmem),
        name="ffn",
    )(s, sh, sc, g_f, norm_pre.reshape(1, d), norm_post.reshape(1, d), w_gate, w_up, w_down)


COL_Z = 3 * GDN_W
COL_HY = 4 * GDN_W


def _split_w_in(w_in):
    d = w_in.shape[0]
    g0 = 3 * GDN_W
    g1 = g0 + 4 * GDN_HEADS
    w_main = jnp.concatenate([w_in[:, :g0].astype(BF16), w_in[:, g1:].astype(BF16)], axis=1)
    w_gate = jnp.zeros((d, LANES), BF16).at[:, :4 * GDN_HEADS].set(w_in[:, g0:g1].astype(BF16))
    return w_main, w_gate


def _row_tile(l, want):
    return want if l % want == 0 else l


def _gdn_branch(x, sh, sc, norm_w, w_main, w_gate, conv_qkv, a_log, dt_bias, s0):
    l = x.shape[1]
    proj, gates = _in_proj(x, sh, sc, norm_w, w_main, w_gate, _row_tile(l, 1024))
    k, v, q, gcol, grow = _gdn_prep(proj, gates, conv_qkv, a_log, dt_bias, _row_tile(l, 256))
    u, w, kd, qg, at, eg = _gdn_chunk(k, v, q, gcol, grow, _row_tile(l, 4 * GDN_CHUNK))
    o_f, o_b, s_fin = _gdn_scan(u, w, kd, qg, at, eg, s0, _row_tile(l, 4 * GDN_CHUNK))
    return proj, o_f, o_b, s_fin


def kernel(x, c, ctx, c_ctx, w_mod, b_mod, norm_pre_mix, norm_post_mix, norm_pre_ffn, norm_post_ffn, w_in, conv_qkv, a_log, dt_bias, gdn_norm, conv_hy, conv_hy_b, filt_w1, filt_b1, filt_freq1, filt_w2, filt_b2, filt_freq2, filt_w3, filt_b3, filt_freq3, filt_w4, hyena_bias, w_out, w_gate, w_up, w_down):
    depth = w_in.shape[0]
    bsz, seq_len, d = x.shape
    assert depth == 1, "the context stream continuation of deeper stacks is not implemented"
    layer = 0
    cvec = jnp.zeros((8, d), F32).at[:bsz].set(c).at[bsz].set(c_ctx)
    mod = _adaln_mod(cvec, w_mod[layer], b_mod[layer])
    sh_a, sc_a, g_a, sh_f, sc_f, g_f = [m.reshape(8, 1, d) for m in jnp.split(mod, 6, axis=-1)]
    ctx_rows = lambda m: jnp.broadcast_to(m[bsz:bsz + 1], (bsz, 1, d))
    w_main, w_gt = _split_w_in(w_in.reshape(w_in.shape[1:]))
    zeros = jnp.zeros((2, bsz, GDN_HEADS, GDN_HEAD_DIM, GDN_HEAD_DIM), F32)
    _, _, _, s_ctx = _gdn_branch(ctx, ctx_rows(sh_a), ctx_rows(sc_a), norm_pre_mix[layer], w_main, w_gt,
                                 conv_qkv[layer], a_log[layer], dt_bias[layer], zeros)
    proj, o_f, o_b, _ = _gdn_branch(x, sh_a[:bsz], sc_a[:bsz], norm_pre_mix[layer], w_main, w_gt,
                                    conv_qkv[layer], a_log[layer], dt_bias[layer], s_ctx)
    tabs = _dft_tables(seq_len)
    hmlp = _hyena_mlp(seq_len, filt_w1[layer], filt_b1[layer], filt_freq1[layer], filt_w2[layer], filt_b2[layer],
                      filt_freq2[layer], filt_w3[layer], filt_b3[layer], filt_freq3[layer])
    kf = _hyena_spec(hmlp, filt_w4[layer], hyena_bias[layer], tabs, seq_len)
    y = _hyena_conv(proj, COL_HY, kf, conv_hy[layer], conv_hy_b[layer], tabs, seq_len)
    as_bf16 = lambda w: w.reshape(w.shape[1:]).astype(BF16)
    s1 = _out_proj(o_f, o_b, proj, COL_Z, y, x, g_a[:bsz], gdn_norm[layer], norm_post_mix[layer],
                   as_bf16(w_out), _row_tile(seq_len, 512))
    return _ffn(s1, sh_f[:bsz], sc_f[:bsz], g_f[:bsz], norm_pre_ffn[layer], norm_post_ffn[layer],
                as_bf16(w_gate), as_bf16(w_up), as_bf16(w_down),
                _row_tile(seq_len, 512), 512)
```
